```python
import jax, jax.numpy as jnp
from jax import lax
import numpy as np

D_MODEL = 2048
BATCH = 4
SEQ = 2048
DEPTH = 4

N_A_LAYERS = DEPTH // 2
N_B_LAYERS = DEPTH - N_A_LAYERS
A_HEADS = 8
A_DV = D_MODEL // A_HEADS
A_DQK = A_DV // 2
A_CHUNK = 64
GATE_SOFTCAP = 15.0
A_IN_COLS = 2 * A_HEADS * A_DQK + 2 * A_HEADS * A_DV + 2 * A_HEADS
B_HEADS = 16
B_DH = D_MODEL // B_HEADS
B_QBLOCK = 128
D_FF = 5632
EPS = 1e-6

kernel_name = "mlstm_stickbreak_yoco_macaron"


def rms_norm(x, g):
    xf = x.astype(jnp.float32)
    y = xf * lax.rsqrt(jnp.mean(xf * xf, axis=-1, keepdims=True) + EPS)
    return (y * g.astype(jnp.float32)).astype(x.dtype)


def swiglu(x, w_in, w_out):
    gate, up = jnp.split(x @ w_in, 2, axis=-1)
    return (jax.nn.silu(gate) * up) @ w_out


def mlstm_mix(h, w_in, b_gate, head_norm, w_out):
    B, S, _ = h.shape
    H, L = A_HEADS, A_CHUNK
    NC = S // L
    proj = h @ w_in
    o1 = H * A_DQK
    o2 = 2 * H * A_DQK
    o3 = o2 + H * A_DV
    o4 = o3 + H * A_DV
    q, k, v, og, gates = proj[..., :o1], proj[..., o1:o2], proj[..., o2:o3], proj[..., o3:o4], proj[..., o4:]
    gates = gates.astype(jnp.float32) + b_gate.astype(jnp.float32)
    ig = GATE_SOFTCAP * jnp.tanh(gates[..., :H] / GATE_SOFTCAP)
    logf = jax.nn.log_sigmoid(gates[..., H:])

    def to_chunks(t, d):
        t = t.astype(jnp.float32).reshape(B, NC, L, H, d)
        return t.transpose(1, 0, 3, 2, 4)

    qc = to_chunks(q, A_DQK) * (A_DQK ** -0.5)
    kc = to_chunks(k, A_DQK)
    vc = to_chunks(v, A_DV)
    igc = ig.reshape(B, NC, L, H).transpose(1, 0, 3, 2)
    lfc = logf.reshape(B, NC, L, H).transpose(1, 0, 3, 2)
    tril = jnp.tril(jnp.ones((L, L), dtype=bool))

    def chunk_step(carry, inp):
        C, n, m = carry
        qq, kk, vv, ii, lf = inp
        b = jnp.cumsum(lf, axis=-1)
        Dm = jnp.where(tril, b[..., :, None] - b[..., None, :] + ii[..., None, :], -jnp.inf)
        m_inter = b + m[..., None]
        m_t = jnp.maximum(jnp.max(Dm, axis=-1), m_inter)
        Sw = jnp.einsum('bhtd,bhsd->bhts', qq, kk) * jnp.exp(Dm - m_t[..., None])
        dec = jnp.exp(m_inter - m_t)
        num = jnp.einsum('bhts,bhsv->bhtv', Sw, vv) + dec[..., None] * jnp.einsum('bhvd,bhtd->bhtv', C, qq)
        den = jnp.sum(Sw, axis=-1) + dec * jnp.einsum('bhd,bhtd->bht', n, qq)
        out = num / jnp.maximum(jnp.abs(den), jnp.exp(-m_t))[..., None]
        bL = b[..., -1]
        g = bL[..., None] - b + ii
        m_new = jnp.maximum(bL + m, jnp.max(g, axis=-1))
        w = jnp.exp(g - m_new[..., None])
        cdec = jnp.exp(bL + m - m_new)
        C = cdec[..., None, None] * C + jnp.einsum('bhs,bhsv,bhsd->bhvd', w, vv, kk)
        n = cdec[..., None] * n + jnp.einsum('bhs,bhsd->bhd', w, kk)
        return (C, n, m_new), out

    init = (jnp.zeros((B, H, A_DV, A_DQK), jnp.float32),
            jnp.zeros((B, H, A_DQK), jnp.float32),
            jnp.zeros((B, H), jnp.float32))
    _, hs = lax.scan(chunk_step, init, (qc, kc, vc, igc, lfc))
    hs = hs.transpose(1, 0, 3, 2, 4).reshape(B, S, H, A_DV)
    hs = hs * lax.rsqrt(jnp.mean(hs * hs, axis=-1, keepdims=True) + EPS)
    hs = hs.reshape(B, S, H * A_DV) * head_norm.astype(jnp.float32)
    hs = hs * jax.nn.sigmoid(og.astype(jnp.float32))
    return hs.astype(h.dtype) @ w_out


def shared_kv(x, g, w_kv):
    B, S, _ = x.shape
    k, v = jnp.split(rms_norm(x, g) @ w_kv, 2, axis=-1)
    k = k.astype(jnp.float32).reshape(B, S, B_HEADS, B_DH).transpose(0, 2, 1, 3)
    v = v.astype(jnp.float32).reshape(B, S, B_HEADS, B_DH).transpose(0, 2, 1, 3)
    return k, v


def stick_breaking_mix(h, w_q, k, v, w_out):
    B, S, _ = h.shape
    q = (h @ w_q).astype(jnp.float32).reshape(B, S, B_HEADS, B_DH).transpose(0, 2, 1, 3) * (B_DH ** -0.5)
    outs = []
    for blk in range(S // B_QBLOCK):
        end = (blk + 1) * B_QBLOCK
        qb = q[:, :, end - B_QBLOCK:end]
        kb = k[:, :, :end]
        vb = v[:, :, :end]
        z = jnp.einsum('bhqd,bhkd->bhqk', qb, kb)
        qpos = jnp.arange(end - B_QBLOCK, end)
        kpos = jnp.arange(end)
        mask = kpos[None, :] < qpos[:, None]
        log_not = jnp.where(mask, jax.nn.log_sigmoid(-z), 0.0)
        after = lax.cumsum(log_not, axis=3, reverse=True) - log_not
        att = jnp.where(mask, jnp.exp(jax.nn.log_sigmoid(z) + after), 0.0)
        outs.append(jnp.einsum('bhqk,bhkd->bhqd', att, vb))
    o = jnp.concatenate(outs, axis=2).transpose(0, 2, 1, 3).reshape(B, S, B_HEADS * B_DH)
    return o.astype(h.dtype) @ w_out


def setup_inputs(seed: int = 0) -> dict:
    key = jax.random.key(seed)
    ks = jax.random.split(key, 20)
    out_scale = (2 * DEPTH) ** -0.5

    def w(k_, shape, fan_in, scale=1.0):
        return jax.random.normal(k_, shape, jnp.float32) * (scale * fan_in ** -0.5)

    def gain(k_, shape):
        return 1.0 + 0.02 * jax.random.normal(k_, shape, jnp.float32)

    gate_bias = jnp.concatenate([
        0.1 * jax.random.normal(ks[17], (N_A_LAYERS, A_HEADS), jnp.float32),
        3.0 + 0.5 * jax.random.normal(ks[18], (N_A_LAYERS, A_HEADS), jnp.float32)], axis=-1)
    return {
        "x": jax.random.normal(ks[0], (BATCH, SEQ, D_MODEL), jnp.float32),
        "ffn1_norm": gain(ks[1], (DEPTH, D_MODEL)),
        "ffn1_w_in": w(ks[2], (DEPTH, D_MODEL, 2 * D_FF), D_MODEL),
        "ffn1_w_out": w(ks[3], (DEPTH, D_FF, D_MODEL), D_FF, out_scale),
        "mix_norm": gain(ks[4], (DEPTH, D_MODEL)),
        "ffn2_norm": gain(ks[5], (DEPTH, D_MODEL)),
        "ffn2_w_in": w(ks[6], (DEPTH, D_MODEL, 2 * D_FF), D_MODEL),
        "ffn2_w_out": w(ks[7], (DEPTH, D_FF, D_MODEL), D_FF, out_scale),
        "a_w_in": w(ks[8], (N_A_LAYERS, D_MODEL, A_IN_COLS), D_MODEL),
        "a_b_gate": gate_bias,
        "a_head_norm": gain(ks[9], (N_A_LAYERS, A_HEADS * A_DV)),
        "a_w_out": w(ks[10], (N_A_LAYERS, A_HEADS * A_DV, D_MODEL), A_HEADS * A_DV, out_scale),
        "kv_norm": gain(ks[11], (D_MODEL,)),
        "kv_w": w(ks[12], (D_MODEL, 2 * B_HEADS * B_DH), D_MODEL),
        "b_w_q": w(ks[13], (N_B_LAYERS, D_MODEL, B_HEADS * B_DH), D_MODEL),
        "b_w_out": w(ks[14], (N_B_LAYERS, B_HEADS * B_DH, D_MODEL), B_HEADS * B_DH, out_scale),
        "final_norm": gain(ks[15], (D_MODEL,)),
    }


def reference(x, ffn1_norm, ffn1_w_in, ffn1_w_out, mix_norm, ffn2_norm, ffn2_w_in, ffn2_w_out,
              a_w_in, a_b_gate, a_head_norm, a_w_out, kv_norm, kv_w, b_w_q, b_w_out, final_norm):
    k_sh, v_sh = None, None
    for l in range(DEPTH):
        x = x + 0.5 * swiglu(rms_norm(x, ffn1_norm[l]), ffn1_w_in[l], ffn1_w_out[l])
        hn = rms_norm(x, mix_norm[l])
        if l < N_A_LAYERS:
            x = x + mlstm_mix(hn, a_w_in[l], a_b_gate[l], a_head_norm[l], a_w_out[l])
        else:
            j = l - N_A_LAYERS
            x = x + stick_breaking_mix(hn, b_w_q[j], k_sh, v_sh, b_w_out[j])
        x = x + 0.5 * swiglu(rms_norm(x, ffn2_norm[l]), ffn2_w_in[l], ffn2_w_out[l])
        if l == N_A_LAYERS - 1:
            k_sh, v_sh = shared_kv(x, kv_norm, kv_w)
    return rms_norm(x, final_norm)
```

```python
import functools

import jax
import jax.numpy as jnp
from jax import lax
from jax.experimental import pallas as pl
from jax.experimental.pallas import tpu as pltpu

EPS = 1e-6
GATE_SOFTCAP = 15.0
A_HEADS = 8
B_HEADS = 16
MLSTM_CHUNK = 256
SB_BLOCK = 128
VMEM_LIMIT_BYTES = 56 * 1024 * 1024

F32 = jnp.float32
BF16 = jnp.bfloat16


def _params(*sem):
    return pltpu.CompilerParams(dimension_semantics=sem, vmem_limit_bytes=VMEM_LIMIT_BYTES)


def _tile(n, want):
    t = min(n, want)
    assert n % t == 0, (n, want)
    return t


def _rms_scale(x, g):
    ms = jnp.mean(x * x, axis=-1, keepdims=True)
    return x * lax.rsqrt(ms + EPS) * g


def _log_sigmoid(x):
    return jnp.minimum(x, 0.0) - jnp.log1p(jnp.exp(-jnp.abs(x)))


def _ffn_body(x_ref, g_ref, wg_ref, wu_ref, wo_ref, o_ref, xn_ref):
    @pl.when(pl.program_id(1) == 0)
    def _():
        x = x_ref[...]
        xn_ref[...] = _rms_scale(x, g_ref[...]).astype(BF16)
        o_ref[...] = x

    xn = xn_ref[...]
    gate = jnp.dot(xn, wg_ref[...].astype(BF16), preferred_element_type=F32)
    up = jnp.dot(xn, wu_ref[...].astype(BF16), preferred_element_type=F32)
    act = (0.5 * gate * jax.nn.sigmoid(gate) * up).astype(BF16)
    o_ref[...] += jnp.dot(act, wo_ref[...].astype(BF16), preferred_element_type=F32)


def _ffn(x, g, w_in, w_out, layer, *, tm=512, tf=512):
    T, D = x.shape
    F = w_out.shape[1]
    tm, tf = _tile(T, tm), _tile(F, tf)
    nf = F // tf
    return pl.pallas_call(
        _ffn_body,
        grid=(T // tm, nf),
        in_specs=[
            pl.BlockSpec((tm, D), lambda i, j: (i, 0)),
            pl.BlockSpec((1, D), lambda i, j: (0, 0)),
            pl.BlockSpec((None, D, tf), lambda i, j: (layer, 0, j)),
            pl.BlockSpec((None, D, tf), lambda i, j: (layer, 0, j + nf)),
            pl.BlockSpec((None, tf, D), lambda i, j: (layer, j, 0)),
        ],
        out_specs=pl.BlockSpec((tm, D), lambda i, j: (i, 0)),
        out_shape=jax.ShapeDtypeStruct((T, D), F32),
        scratch_shapes=[pltpu.VMEM((tm, D), BF16)],
        compiler_params=_params("parallel", "arbitrary"),
        name="ffn",
    )(x, g.reshape(1, D), w_in, w_in, w_out)


def _norm_matmul_body(x_ref, g_ref, w_ref, o_ref, xn_ref):
    @pl.when(pl.program_id(1) == 0)
    def _():
        xn_ref[...] = _rms_scale(x_ref[...], g_ref[...]).astype(BF16)

    o_ref[...] = jnp.dot(xn_ref[...], w_ref[...].astype(BF16),
                         preferred_element_type=F32).astype(o_ref.dtype)


def _norm_matmul(x, g, w, layer, col0, n, out_dtype, *, tm=512, tn=512):
    T, D = x.shape
    tm, tn = _tile(T, tm), _tile(n, tn)
    assert col0 % tn == 0
    cb = col0 // tn
    return pl.pallas_call(
        _norm_matmul_body,
        grid=(T // tm, n // tn),
        in_specs=[
            pl.BlockSpec((tm, D), lambda i, j: (i, 0)),
            pl.BlockSpec((1, D), lambda i, j: (0, 0)),
            pl.BlockSpec((None, D, tn), lambda i, j: (layer, 0, j + cb)),
        ],
        out_specs=pl.BlockSpec((tm, tn), lambda i, j: (i, j)),
        out_shape=jax.ShapeDtypeStruct((T, n), out_dtype),
        scratch_shapes=[pltpu.VMEM((tm, D), BF16)],
        compiler_params=_params("parallel", "arbitrary"),
        name="norm_matmul",
    )(x, g.reshape(1, D), w)


def _matmul_residual_body(h_ref, w_ref, x_ref, o_ref):
    o_ref[...] = x_ref[...] + jnp.dot(h_ref[...], w_ref[...].astype(BF16),
                                      preferred_element_type=F32)


def _matmul_residual(h, w, layer, x, *, tm=512, tn=512):
    T, K = h.shape
    D = x.shape[1]
    tm, tn = _tile(T, tm), _tile(D, tn)
    return pl.pallas_call(
        _matmul_residual_body,
        grid=(T // tm, D // tn),
        in_specs=[
            pl.BlockSpec((tm, K), lambda i, j: (i, 0)),
            pl.BlockSpec((None, K, tn), lambda i, j: (layer, 0, j)),
            pl.BlockSpec((tm, tn), lambda i, j: (i, j)),
        ],
        out_specs=pl.BlockSpec((tm, tn), lambda i, j: (i, j)),
        out_shape=jax.ShapeDtypeStruct((T, D), F32),
        compiler_params=_params("parallel", "arbitrary"),
        name="matmul_residual",
    )(h, w, x)


def _split3(x):
    h1 = x.astype(BF16)
    r1 = x - h1.astype(F32)
    h2 = r1.astype(BF16)
    h3 = (r1 - h2.astype(F32)).astype(BF16)
    return h1, h2, h3


def _mlstm_body(q_ref, k_ref, v_ref, og_ref, gc_ref, gr_ref, bc_ref, br_ref, hn_ref,
                o_ref, c_ref, n_ref, m_ref, *, L, H):
    c = pl.program_id(1)
    h = pl.program_id(2)
    dqk = q_ref.shape[1]
    scale = dqk ** -0.5

    @pl.when(c == 0)
    def _():
        c_ref[h] = jnp.zeros(c_ref.shape[1:], F32)
        n_ref[h] = jnp.zeros(n_ref.shape[1:], F32)
        m_ref[h] = jnp.zeros(m_ref.shape[1:], F32)

    row = lax.broadcasted_iota(jnp.int32, (L, L), 0)
    col = lax.broadcasted_iota(jnp.int32, (L, L), 1)
    causal = col <= row
    tri = causal.astype(BF16)

    gcol = gc_ref[...] + bc_ref[...]
    lane = lax.broadcasted_iota(jnp.int32, gcol.shape, 1)
    ig_c = GATE_SOFTCAP * jnp.tanh(gcol * (1.0 / GATE_SOFTCAP))
    lf_c = _log_sigmoid(gcol)
    icol = jnp.sum(jnp.where(lane == h, ig_c, 0.0), axis=1, keepdims=True)
    lfcol = jnp.sum(jnp.where(lane == h + H, lf_c, 0.0), axis=1, keepdims=True)
    lf_b = jnp.broadcast_to(lfcol, (L, 128))
    bcol = sum(jnp.dot(tri, p, preferred_element_type=F32) for p in _split3(lf_b))[:, :1]

    grow = gr_ref[...] + br_ref[...]
    sub = lax.broadcasted_iota(jnp.int32, grow.shape, 0)
    ig_r = GATE_SOFTCAP * jnp.tanh(grow * (1.0 / GATE_SOFTCAP))
    lf_r = _log_sigmoid(grow)
    irow = jnp.sum(jnp.where(sub == h, ig_r, 0.0), axis=0, keepdims=True)
    b_all = sum(lax.dot_general(p, tri, (((1,), (1,)), ((), ())), preferred_element_type=F32)
                for p in _split3(lf_r))
    brow = jnp.sum(jnp.where(sub == h + H, b_all, 0.0), axis=0, keepdims=True)

    m_prev = m_ref[h]
    dm = jnp.where(causal, bcol - brow + irow, -jnp.inf)
    m_inter = bcol + m_prev
    m_t = jnp.maximum(jnp.max(dm, axis=1, keepdims=True), m_inter)

    q = q_ref[...]
    k = k_ref[...]
    v = v_ref[...]
    s = lax.dot_general(q, k, (((1,), (1,)), ((), ())), preferred_element_type=F32) * scale
    sw = s * jnp.exp(dm - m_t)
    dec = jnp.exp(m_inter - m_t)
    ct = c_ref[h]
    n_prev = n_ref[h]
    qc = jnp.dot(q, ct.astype(BF16), preferred_element_type=F32) * scale
    num = jnp.dot(sw.astype(BF16), v, preferred_element_type=F32) + dec * qc
    nq = jnp.sum(q.astype(F32) * n_prev, axis=1, keepdims=True) * scale
    den = jnp.sum(sw, axis=1, keepdims=True) + dec * nq
    out = num / jnp.maximum(jnp.abs(den), jnp.exp(-m_t))

    hs = out * lax.rsqrt(jnp.mean(out * out, axis=1, keepdims=True) + EPS)
    hs = hs * hn_ref[...] * jax.nn.sigmoid(og_ref[...])
    o_ref[...] = hs.astype(o_ref.dtype)

    b_last = bcol[L - 1:L, :]
    gdec = b_last - bcol + icol
    m_new = jnp.maximum(b_last + m_prev, jnp.max(gdec, axis=0, keepdims=True))
    w = jnp.exp(gdec - m_new)
    cdec = jnp.exp(b_last + m_prev - m_new)
    wv = (w * v.astype(F32)).astype(BF16)
    kv = lax.dot_general(k, wv, (((0,), (0,)), ((), ())), preferred_element_type=F32)
    c_ref[h] = cdec * ct + kv
    n_ref[h] = cdec * n_prev + jnp.sum(w * k.astype(F32), axis=0, keepdims=True)
    m_ref[h] = m_new


def _mlstm_core(qkv, og, gates, b_gate, head_norm, B, S):
    H = A_HEADS
    T = B * S
    dv = og.shape[1] // H
    dqk = dv // 2
    L = _tile(S, MLSTM_CHUNK)
    nc = S // L
    gates_t = gates.T
    kern = functools.partial(_mlstm_body, L=L, H=H)
    return pl.pallas_call(
        kern,
        grid=(B, nc, H),
        in_specs=[
            pl.BlockSpec((L, dqk), lambda b, c, h: (b * nc + c, h)),
            pl.BlockSpec((L, dqk), lambda b, c, h: (b * nc + c, H + h)),
            pl.BlockSpec((L, dv), lambda b, c, h: (b * nc + c, H + h)),
            pl.BlockSpec((L, dv), lambda b, c, h: (b * nc + c, h)),
            pl.BlockSpec((L, 2 * H), lambda b, c, h: (b * nc + c, 0)),
            pl.BlockSpec((2 * H, L), lambda b, c, h: (0, b * nc + c)),
            pl.BlockSpec((1, 2 * H), lambda b, c, h: (0, 0)),
            pl.BlockSpec((2 * H, 1), lambda b, c, h: (0, 0)),
            pl.BlockSpec((1, dv), lambda b, c, h: (0, h)),
        ],
        out_specs=pl.BlockSpec((L, dv), lambda b, c, h: (b * nc + c, h)),
        out_shape=jax.ShapeDtypeStruct((T, H * dv), BF16),
        scratch_shapes=[
            pltpu.VMEM((H, dqk, dv), F32),
            pltpu.VMEM((H, 1, dqk), F32),
            pltpu.VMEM((H, 1, 1), F32),
        ],
        compiler_params=_params("parallel", "arbitrary", "arbitrary"),
        name="mlstm_core",
    )(qkv, qkv, qkv, og, gates, gates_t, b_gate.reshape(1, 2 * H), b_gate.reshape(2 * H, 1),
      head_norm.reshape(1, H * dv))


def _sb_body(q_ref, k_ref, v_ref, o_ref, *, QB):
    qi = pl.program_id(2)
    dh = q_ref.shape[1]
    scale = dh ** -0.5
    q = q_ref[...]
    row = lax.broadcasted_iota(jnp.int32, (QB, QB), 0)
    col = lax.broadcasted_iota(jnp.int32, (QB, QB), 1)
    later = (row > col).astype(BF16)
    strict = col < row

    def tile(j, carry, acc, diag):
        start = pl.multiple_of(j * QB, QB)
        kb = k_ref[pl.ds(start, QB), :]
        vb = v_ref[pl.ds(start, QB), :]
        z = lax.dot_general(q, kb, (((1,), (1,)), ((), ())), preferred_element_type=F32) * scale
        sp = jnp.maximum(z, 0.0) + jnp.log1p(jnp.exp(-jnp.abs(z)))
        log_not = -sp
        if diag:
            log_not = jnp.where(strict, log_not, 0.0)
        hi = log_not.astype(BF16)
        lo = (log_not - hi.astype(F32)).astype(BF16)
        after = (jnp.dot(hi, later, preferred_element_type=F32)
                 + jnp.dot(lo, later, preferred_element_type=F32) + carry)
        att = jnp.exp(z - sp + after)
        if diag:
            att = jnp.where(strict, att, 0.0)
        acc = acc + jnp.dot(att.astype(BF16), vb, preferred_element_type=F32)
        carry = carry + jnp.sum(log_not, axis=1, keepdims=True)
        return carry, acc

    carry, acc = tile(qi, jnp.zeros((QB, 1), F32), jnp.zeros((QB, dh), F32), True)
    carry, acc = lax.fori_loop(
        0, qi, lambda it, ca: tile(qi - 1 - it, ca[0], ca[1], False), (carry, acc))
    o_ref[...] = acc.astype(o_ref.dtype)


def _stick_breaking_core(q, kv, B, S):
    H = B_HEADS
    T = B * S
    dh = q.shape[1] // H
    QB = _tile(S, SB_BLOCK)
    nq = S // QB
    kern = functools.partial(_sb_body, QB=QB)
    return pl.pallas_call(
        kern,
        grid=(B, H, nq),
        in_specs=[
            pl.BlockSpec((QB, dh), lambda b, h, i: (b * nq + i, h)),
            pl.BlockSpec((S, dh), lambda b, h, i: (b, h)),
            pl.BlockSpec((S, dh), lambda b, h, i: (b, H + h)),
        ],
        out_specs=pl.BlockSpec((QB, dh), lambda b, h, i: (b * nq + i, h)),
        out_shape=jax.ShapeDtypeStruct((T, H * dh), BF16),
        compiler_params=_params("parallel", "parallel", "arbitrary"),
        name="stick_breaking_core",
    )(q, kv, kv)


def _rms_body(x_ref, g_ref, o_ref):
    o_ref[...] = _rms_scale(x_ref[...], g_ref[...])


def _rms_norm(x, g, *, tm=512):
    T, D = x.shape
    tm = _tile(T, tm)
    return pl.pallas_call(
        _rms_body,
        grid=(T // tm,),
        in_specs=[pl.BlockSpec((tm, D), lambda i: (i, 0)),
                  pl.BlockSpec((1, D), lambda i: (0, 0))],
        out_specs=pl.BlockSpec((tm, D), lambda i: (i, 0)),
        out_shape=jax.ShapeDtypeStruct((T, D), F32),
        compiler_params=_params("parallel"),
        name="final_rms_norm",
    )(x, g.reshape(1, D))


def kernel(x, ffn1_norm, ffn1_w_in, ffn1_w_out, mix_norm, ffn2_norm, ffn2_w_in, ffn2_w_out,
           a_w_in, a_b_gate, a_head_norm, a_w_out, kv_norm, kv_w, b_w_q, b_w_out, final_norm):
    B, S, D = x.shape
    depth = ffn1_norm.shape[0]
    n_a = a_w_in.shape[0]
    n_qk = A_HEADS * (D // A_HEADS // 2)
    n_qkv = 2 * n_qk + D
    xt = x.reshape(B * S, D)
    kv = None
    for l in range(depth):
        xt = _ffn(xt, ffn1_norm[l], ffn1_w_in, ffn1_w_out, l)
        if l < n_a:
            qkv = _norm_matmul(xt, mix_norm[l], a_w_in, l, 0, n_qkv, BF16)
            og = _norm_matmul(xt, mix_norm[l], a_w_in, l, n_qkv, D, F32)
            w_gate = a_w_in[l, :, n_qkv + D:][None]
            gates = _norm_matmul(xt, mix_norm[l], w_gate, 0, 0, 2 * A_HEADS, F32)
            hs = _mlstm_core(qkv, og, gates, a_b_gate[l], a_head_norm[l], B, S)
            xt = _matmul_residual(hs, a_w_out, l, xt)
        else:
            j = l - n_a
            q = _norm_matmul(xt, mix_norm[l], b_w_q, j, 0, D, BF16)
            o = _stick_breaking_core(q, kv, B, S)
            xt = _matmul_residual(o, b_w_out, j, xt)
        xt = _ffn(xt, ffn2_norm[l], ffn2_w_in, ffn2_w_out, l)
        if l == n_a - 1:
            kv = _norm_matmul(xt, kv_norm, kv_w[None], 0, 0, 2 * D, BF16)
    return _rms_norm(xt, final_norm).reshape(B, S, D)
```

```python
import functools

import jax
import jax.numpy as jnp
from jax import lax
from jax.experimental import pallas as pl
from jax.experimental.pallas import tpu as pltpu

EPS = 1e-6
GATE_SOFTCAP = 15.0
A_HEADS = 8
B_HEADS = 16
MLSTM_CHUNK = 256
SB_BLOCK = 256
SB_HEADS_PER_STEP = 4
VMEM_LIMIT_BYTES = 56 * 1024 * 1024

F32 = jnp.float32
BF16 = jnp.bfloat16


def _params(*sem):
    return pltpu.CompilerParams(dimension_semantics=sem, vmem_limit_bytes=VMEM_LIMIT_BYTES)


def _tile(n, want):
    t = min(n, want)
    assert n % t == 0, (n, want)
    return t


def _rms_scale(x, g):
    ms = jnp.mean(x * x, axis=-1, keepdims=True)
    return x * lax.rsqrt(ms + EPS) * g


def _log_sigmoid(x):
    return jnp.minimum(x, 0.0) - jnp.log1p(jnp.exp(-jnp.abs(x)))


def _ffn_body(x_ref, g_ref, wg_ref, wu_ref, wo_ref, o_ref, xn_ref):
    @pl.when(pl.program_id(1) == 0)
    def _():
        x = x_ref[...]
        xn_ref[...] = _rms_scale(x, g_ref[...]).astype(BF16)
        o_ref[...] = x

    xn = xn_ref[...]
    gate = jnp.dot(xn, wg_ref[...].astype(BF16), preferred_element_type=F32)
    up = jnp.dot(xn, wu_ref[...].astype(BF16), preferred_element_type=F32)
    act = (0.5 * gate * jax.nn.sigmoid(gate) * up).astype(BF16)
    o_ref[...] += jnp.dot(act, wo_ref[...].astype(BF16), preferred_element_type=F32)


def _ffn(x, g, w_in, w_out, layer, *, tm=1024, tf=256):
    T, D = x.shape
    F = w_out.shape[1]
    tm, tf = _tile(T, tm), _tile(F, tf)
    nf = F // tf
    return pl.pallas_call(
        _ffn_body,
        grid=(T // tm, nf),
        in_specs=[
            pl.BlockSpec((tm, D), lambda i, j: (i, 0), pipeline_mode=pl.Buffered(1)),
            pl.BlockSpec((1, D), lambda i, j: (0, 0)),
            pl.BlockSpec((None, D, tf), lambda i, j: (layer, 0, j)),
            pl.BlockSpec((None, D, tf), lambda i, j: (layer, 0, j + nf)),
            pl.BlockSpec((None, tf, D), lambda i, j: (layer, j, 0)),
        ],
        out_specs=pl.BlockSpec((tm, D), lambda i, j: (i, 0)),
        out_shape=jax.ShapeDtypeStruct((T, D), F32),
        scratch_shapes=[pltpu.VMEM((tm, D), BF16)],
        compiler_params=_params("parallel", "arbitrary"),
        name="ffn",
    )(x, g.reshape(1, D), w_in, w_in, w_out)


def _norm_matmul_body(x_ref, g_ref, w_ref, o_ref, xn_ref, *, scale):
    @pl.when(pl.program_id(1) == 0)
    def _():
        xn_ref[...] = _rms_scale(x_ref[...], g_ref[...]).astype(BF16)

    y = jnp.dot(xn_ref[...], w_ref[...].astype(BF16), preferred_element_type=F32)
    if scale != 1.0:
        y = y * scale
    o_ref[...] = y.astype(o_ref.dtype)


def _norm_matmul(x, g, w, layer, col0, n, out_dtype, *, scale=1.0, tm=1024, tn=512):
    T, D = x.shape
    tm, tn = _tile(T, tm), _tile(n, tn)
    assert col0 % tn == 0
    cb = col0 // tn
    return pl.pallas_call(
        functools.partial(_norm_matmul_body, scale=float(scale)),
        grid=(T // tm, n // tn),
        in_specs=[
            pl.BlockSpec((tm, D), lambda i, j: (i, 0)),
            pl.BlockSpec((1, D), lambda i, j: (0, 0)),
            pl.BlockSpec((None, D, tn), lambda i, j: (layer, 0, j + cb)),
        ],
        out_specs=pl.BlockSpec((tm, tn), lambda i, j: (i, j)),
        out_shape=jax.ShapeDtypeStruct((T, n), out_dtype),
        scratch_shapes=[pltpu.VMEM((tm, D), BF16)],
        compiler_params=_params("parallel", "arbitrary"),
        name="norm_matmul",
    )(x, g.reshape(1, D), w)


def _matmul_residual_body(h_ref, w_ref, x_ref, o_ref):
    o_ref[...] = x_ref[...] + jnp.dot(h_ref[...], w_ref[...].astype(BF16),
                                      preferred_element_type=F32)


def _matmul_residual(h, w, layer, x, *, tm=1024, tn=512):
    T, K = h.shape
    D = x.shape[1]
    tm, tn = _tile(T, tm), _tile(D, tn)
    return pl.pallas_call(
        _matmul_residual_body,
        grid=(T // tm, D // tn),
        in_specs=[
            pl.BlockSpec((tm, K), lambda i, j: (i, 0)),
            pl.BlockSpec((None, K, tn), lambda i, j: (layer, 0, j)),
            pl.BlockSpec((tm, tn), lambda i, j: (i, j)),
        ],
        out_specs=pl.BlockSpec((tm, tn), lambda i, j: (i, j)),
        out_shape=jax.ShapeDtypeStruct((T, D), F32),
        compiler_params=_params("parallel", "arbitrary"),
        name="matmul_residual",
    )(h, w, x)


def _split3(x):
    h1 = x.astype(BF16)
    r1 = x - h1.astype(F32)
    h2 = r1.astype(BF16)
    h3 = (r1 - h2.astype(F32)).astype(BF16)
    return h1, h2, h3


def _mlstm_body(q_ref, k_ref, v_ref, og_ref, gc_ref, gr_ref, bc_ref, br_ref, hn_ref,
                o_ref, c_ref, n_ref, m_ref, *, L, H):
    c = pl.program_id(1)
    h = pl.program_id(2)
    dqk = q_ref.shape[1]
    scale = dqk ** -0.5

    @pl.when(c == 0)
    def _():
        c_ref[h] = jnp.zeros(c_ref.shape[1:], F32)
        n_ref[h] = jnp.zeros(n_ref.shape[1:], F32)
        m_ref[h] = jnp.zeros(m_ref.shape[1:], F32)

    row = lax.broadcasted_iota(jnp.int32, (L, L), 0)
    col = lax.broadcasted_iota(jnp.int32, (L, L), 1)
    causal = col <= row
    tri = causal.astype(BF16)

    gcol = gc_ref[...] + bc_ref[...]
    lane = lax.broadcasted_iota(jnp.int32, gcol.shape, 1)
    ig_c = GATE_SOFTCAP * jnp.tanh(gcol * (1.0 / GATE_SOFTCAP))
    lf_c = _log_sigmoid(gcol)
    icol = jnp.sum(jnp.where(lane == h, ig_c, 0.0), axis=1, keepdims=True)
    lfcol = jnp.sum(jnp.where(lane == h + H, lf_c, 0.0), axis=1, keepdims=True)
    lf_b = jnp.broadcast_to(lfcol, (L, 128))
    bcol = sum(jnp.dot(tri, p, preferred_element_type=F32) for p in _split3(lf_b))[:, :1]

    grow = gr_ref[...] + br_ref[...]
    sub = lax.broadcasted_iota(jnp.int32, grow.shape, 0)
    ig_r = GATE_SOFTCAP * jnp.tanh(grow * (1.0 / GATE_SOFTCAP))
    lf_r = _log_sigmoid(grow)
    irow = jnp.sum(jnp.where(sub == h, ig_r, 0.0), axis=0, keepdims=True)
    b_all = sum(lax.dot_general(p, tri, (((1,), (1,)), ((), ())), preferred_element_type=F32)
                for p in _split3(lf_r))
    brow = jnp.sum(jnp.where(sub == h + H, b_all, 0.0), axis=0, keepdims=True)

    m_prev = m_ref[h]
    dm = jnp.where(causal, bcol - brow + irow, -jnp.inf)
    m_inter = bcol + m_prev
    m_t = jnp.maximum(jnp.max(dm, axis=1, keepdims=True), m_inter)

    q = q_ref[...]
    k = k_ref[...]
    v = v_ref[...]
    s = lax.dot_general(q, k, (((1,), (1,)), ((), ())), preferred_element_type=F32) * scale
    sw = s * jnp.exp(dm - m_t)
    dec = jnp.exp(m_inter - m_t)
    ct = c_ref[h]
    n_prev = n_ref[h]
    qc = jnp.dot(q, ct.astype(BF16), preferred_element_type=F32) * scale
    num = jnp.dot(sw.astype(BF16), v, preferred_element_type=F32) + dec * qc
    nq = jnp.sum(q.astype(F32) * n_prev, axis=1, keepdims=True) * scale
    den = jnp.sum(sw, axis=1, keepdims=True) + dec * nq
    out = num / jnp.maximum(jnp.abs(den), jnp.exp(-m_t))

    hs = out * lax.rsqrt(jnp.mean(out * out, axis=1, keepdims=True) + EPS)
    hs = hs * hn_ref[...] * jax.nn.sigmoid(og_ref[...])
    o_ref[...] = hs.astype(o_ref.dtype)

    b_last = bcol[L - 1:L, :]
    gdec = b_last - bcol + icol
    m_new = jnp.maximum(b_last + m_prev, jnp.max(gdec, axis=0, keepdims=True))
    w = jnp.exp(gdec - m_new)
    cdec = jnp.exp(b_last + m_prev - m_new)
    wv = (w * v.astype(F32)).astype(BF16)
    kv = lax.dot_general(k, wv, (((0,), (0,)), ((), ())), preferred_element_type=F32)
    c_ref[h] = cdec * ct + kv
    n_ref[h] = cdec * n_prev + jnp.sum(w * k.astype(F32), axis=0, keepdims=True)
    m_ref[h] = m_new


def _mlstm_core(qkv, og, gates, b_gate, head_norm, B, S):
    H = A_HEADS
    T = B * S
    dv = og.shape[1] // H
    dqk = dv // 2
    L = _tile(S, MLSTM_CHUNK)
    nc = S // L
    gates_t = gates.T
    kern = functools.partial(_mlstm_body, L=L, H=H)
    return pl.pallas_call(
        kern,
        grid=(B, nc, H),
        in_specs=[
            pl.BlockSpec((L, dqk), lambda b, c, h: (b * nc + c, h)),
            pl.BlockSpec((L, dqk), lambda b, c, h: (b * nc + c, H + h)),
            pl.BlockSpec((L, dv), lambda b, c, h: (b * nc + c, H + h)),
            pl.BlockSpec((L, dv), lambda b, c, h: (b * nc + c, h)),
            pl.BlockSpec((L, 2 * H), lambda b, c, h: (b * nc + c, 0)),
            pl.BlockSpec((2 * H, L), lambda b, c, h: (0, b * nc + c)),
            pl.BlockSpec((1, 2 * H), lambda b, c, h: (0, 0)),
            pl.BlockSpec((2 * H, 1), lambda b, c, h: (0, 0)),
            pl.BlockSpec((1, dv), lambda b, c, h: (0, h)),
        ],
        out_specs=pl.BlockSpec((L, dv), lambda b, c, h: (b * nc + c, h)),
        out_shape=jax.ShapeDtypeStruct((T, H * dv), BF16),
        scratch_shapes=[
            pltpu.VMEM((H, dqk, dv), F32),
            pltpu.VMEM((H, 1, dqk), F32),
            pltpu.VMEM((H, 1, 1), F32),
        ],
        compiler_params=_params("parallel", "arbitrary", "arbitrary"),
        name="mlstm_core",
    )(qkv, qkv, qkv, og, gates, gates_t, b_gate.reshape(1, 2 * H), b_gate.reshape(2 * H, 1),
      head_norm.reshape(1, H * dv))


def _sb_body(q_ref, k_ref, v_ref, o_ref, *, QB, G, dh):
    qi = pl.program_id(2)
    row = lax.broadcasted_iota(jnp.int32, (QB, QB), 0)
    col = lax.broadcasted_iota(jnp.int32, (QB, QB), 1)
    strict = col < row
    later = (row > col).astype(BF16)
    later2 = jnp.concatenate([later, later], axis=0)

    def step(j, state, diag):
        start = pl.multiple_of(j * QB, QB)
        heads = [slice(g * dh, (g + 1) * dh) for g in range(G)]
        zs = [lax.dot_general(q_ref[:, hs], k_ref[pl.ds(start, QB), hs],
                              (((1,), (1,)), ((), ())), preferred_element_type=F32)
              for hs in heads]
        log_sigs, sps, pieces = [], [], []
        for z in zs:
            neg = jnp.minimum(z, 0.0)
            pos = jnp.maximum(z, 0.0)
            lg = jnp.log(1.0 + jnp.exp(neg - pos))
            sp = pos + lg
            if diag:
                sp = jnp.where(strict, sp, 0.0)
            hi = sp.astype(BF16)
            lo = (sp - hi.astype(F32)).astype(BF16)
            log_sigs.append(neg - lg)
            sps.append(sp)
            pieces.append(jnp.concatenate([hi, lo], axis=1))
        s_afters = [jnp.dot(p, later2, preferred_element_type=F32) for p in pieces]
        atts = []
        for g in range(G):
            att = jnp.exp(log_sigs[g] - (s_afters[g] + state[g][0]))
            if diag:
                att = jnp.where(strict, att, 0.0)
            atts.append(att.astype(BF16))
        new_state = []
        for g in range(G):
            acc = state[g][1] + jnp.dot(atts[g], v_ref[pl.ds(start, QB), heads[g]],
                                        preferred_element_type=F32)
            carry = state[g][0] + jnp.sum(sps[g], axis=1, keepdims=True)
            new_state.append((carry, acc))
        return tuple(new_state)

    state = tuple((jnp.zeros((QB, 1), F32), jnp.zeros((QB, dh), F32)) for _ in range(G))
    state = step(qi, state, True)
    state = lax.fori_loop(0, qi, lambda it, st: step(qi - 1 - it, st, False), state)
    o_ref[...] = jnp.concatenate([st[1] for st in state], axis=1).astype(o_ref.dtype)


def _stick_breaking_core(q, kv, B, S):
    H = B_HEADS
    G = SB_HEADS_PER_STEP
    T = B * S
    dh = q.shape[1] // H
    QB = _tile(S, SB_BLOCK)
    nq = S // QB
    kern = functools.partial(_sb_body, QB=QB, G=G, dh=dh)
    return pl.pallas_call(
        kern,
        grid=(B, H // G, nq),
        in_specs=[
            pl.BlockSpec((QB, G * dh), lambda b, h, i: (b * nq + i, h)),
            pl.BlockSpec((S, G * dh), lambda b, h, i: (b, h)),
            pl.BlockSpec((S, G * dh), lambda b, h, i: (b, H // G + h)),
        ],
        out_specs=pl.BlockSpec((QB, G * dh), lambda b, h, i: (b * nq + i, h)),
        out_shape=jax.ShapeDtypeStruct((T, H * dh), BF16),
        compiler_params=_params("parallel", "parallel", "arbitrary"),
        name="stick_breaking_core",
    )(q, kv, kv)


def _rms_body(x_ref, g_ref, o_ref):
    o_ref[...] = _rms_scale(x_ref[...], g_ref[...])


def _rms_norm(x, g, *, tm=512):
    T, D = x.shape
    tm = _tile(T, tm)
    return pl.pallas_call(
        _rms_body,
        grid=(T // tm,),
        in_specs=[pl.BlockSpec((tm, D), lambda i: (i, 0)),
                  pl.BlockSpec((1, D), lambda i: (0, 0))],
        out_specs=pl.BlockSpec((tm, D), lambda i: (i, 0)),
        out_shape=jax.ShapeDtypeStruct((T, D), F32),
        compiler_params=_params("parallel"),
        name="final_rms_norm",
    )(x, g.reshape(1, D))


def kernel(x, ffn1_norm, ffn1_w_in, ffn1_w_out, mix_norm, ffn2_norm, ffn2_w_in, ffn2_w_out,
           a_w_in, a_b_gate, a_head_norm, a_w_out, kv_norm, kv_w, b_w_q, b_w_out, final_norm):
    B, S, D = x.shape
    depth = ffn1_norm.shape[0]
    n_a = a_w_in.shape[0]
    n_qk = A_HEADS * (D // A_HEADS // 2)
    n_qkv = 2 * n_qk + D
    xt = x.reshape(B * S, D)
    kv = None
    for l in range(depth):
        xt = _ffn(xt, ffn1_norm[l], ffn1_w_in, ffn1_w_out, l)
        if l < n_a:
            qkv = _norm_matmul(xt, mix_norm[l], a_w_in, l, 0, n_qkv, BF16)
            og = _norm_matmul(xt, mix_norm[l], a_w_in, l, n_qkv, D, F32)
            w_gate = a_w_in[l, :, n_qkv + D:][None]
            gates = _norm_matmul(xt, mix_norm[l], w_gate, 0, 0, 2 * A_HEADS, F32)
            hs = _mlstm_core(qkv, og, gates, a_b_gate[l], a_head_norm[l], B, S)
            xt = _matmul_residual(hs, a_w_out, l, xt)
        else:
            j = l - n_a
            q = _norm_matmul(xt, mix_norm[l], b_w_q, j, 0, D, BF16, scale=(D // B_HEADS) ** -0.5)
            o = _stick_breaking_core(q, kv, B, S)
            xt = _matmul_residual(o, b_w_out, j, xt)
        xt = _ffn(xt, ffn2_norm[l], ffn2_w_in, ffn2_w_out, l)
        if l == n_a - 1:
            kv = _norm_matmul(xt, kv_norm, kv_w[None], 0, 0, 2 * D, BF16)
    return _rms_norm(xt, final_norm).reshape(B, S, D)
```

```python
import functools

import jax
import jax.numpy as jnp
from jax import lax
from jax.experimental import pallas as pl
from jax.experimental.pallas import tpu as pltpu

EPS = 1e-6
GATE_SOFTCAP = 15.0
A_HEADS = 8
B_HEADS = 16
MLSTM_CHUNK = 256
SB_BLOCK = 256
SB_HEADS_PER_STEP = 4
LANES = 128
VMEM_LIMIT_BYTES = 56 * 1024 * 1024

F32 = jnp.float32
BF16 = jnp.bfloat16


def _params(*sem):
    return pltpu.CompilerParams(dimension_semantics=sem, vmem_limit_bytes=VMEM_LIMIT_BYTES)


def _tile(n, want):
    t = min(n, want)
    assert n % t == 0, (n, want)
    return t


def _rms_scale(x, g):
    ms = jnp.mean(x * x, axis=-1, keepdims=True)
    return x * lax.rsqrt(ms + EPS) * g


def _log_sigmoid(x):
    return jnp.minimum(x, 0.0) - jnp.log1p(jnp.exp(-jnp.abs(x)))


def _ffn_body(x_ref, g_ref, wg_ref, wu_ref, wo_ref, *rest, has_out_norm):
    if has_out_norm:
        go_ref, o_ref, xn_ref = rest
    else:
        o_ref, xn_ref = rest

    @pl.when(pl.program_id(1) == 0)
    def _():
        x = x_ref[...]
        xn_ref[...] = _rms_scale(x, g_ref[...]).astype(BF16)
        o_ref[...] = x

    xn = xn_ref[...]
    gate = jnp.dot(xn, wg_ref[...].astype(BF16), preferred_element_type=F32)
    up = jnp.dot(xn, wu_ref[...].astype(BF16), preferred_element_type=F32)
    act = (0.5 * gate * jax.nn.sigmoid(gate) * up).astype(BF16)
    o_ref[...] += jnp.dot(act, wo_ref[...].astype(BF16), preferred_element_type=F32)

    if has_out_norm:
        @pl.when(pl.program_id(1) == pl.num_programs(1) - 1)
        def _():
            o_ref[...] = _rms_scale(o_ref[...], go_ref[...])


def _ffn(x, g, w_in, w_out, layer, out_norm=None, *, tm=1024, tf=256):
    T, D = x.shape
    F = w_out.shape[1]
    tm, tf = _tile(T, tm), _tile(F, tf)
    nf = F // tf
    extra_specs, extra_args = [], []
    if out_norm is not None:
        extra_specs = [pl.BlockSpec((1, D), lambda i, j: (0, 0))]
        extra_args = [out_norm.reshape(1, D)]
    return pl.pallas_call(
        functools.partial(_ffn_body, has_out_norm=out_norm is not None),
        grid=(T // tm, nf),
        in_specs=[
            pl.BlockSpec((tm, D), lambda i, j: (i, 0), pipeline_mode=pl.Buffered(1)),
            pl.BlockSpec((1, D), lambda i, j: (0, 0)),
            pl.BlockSpec((None, D, tf), lambda i, j: (layer, 0, j)),
            pl.BlockSpec((None, D, tf), lambda i, j: (layer, 0, j + nf)),
            pl.BlockSpec((None, tf, D), lambda i, j: (layer, j, 0)),
        ] + extra_specs,
        out_specs=pl.BlockSpec((tm, D), lambda i, j: (i, 0)),
        out_shape=jax.ShapeDtypeStruct((T, D), F32),
        scratch_shapes=[pltpu.VMEM((tm, D), BF16)],
        compiler_params=_params("parallel", "arbitrary"),
        name="ffn",
    )(x, g.reshape(1, D), w_in, w_in, w_out, *extra_args)


def _norm_matmul_body(x_ref, g_ref, w_ref, o_ref, xn_ref, *, scale, scale_blocks):
    @pl.when(pl.program_id(1) == 0)
    def _():
        xn_ref[...] = _rms_scale(x_ref[...], g_ref[...]).astype(BF16)

    y = jnp.dot(xn_ref[...], w_ref[...].astype(BF16), preferred_element_type=F32)
    if scale != 1.0:
        y = y * jnp.where(pl.program_id(1) < scale_blocks, scale, 1.0)
    o_ref[...] = y.astype(o_ref.dtype)


def _norm_matmul(x, g, w, layer, col0, n, out_dtype, *, scale=1.0, scale_cols=None,
                 tm=1024, tn=1024):
    T, D = x.shape
    tm, tn = _tile(T, tm), _tile(n, tn)
    scale_cols = n if scale_cols is None else scale_cols
    assert col0 % tn == 0 and scale_cols % tn == 0
    cb = col0 // tn
    return pl.pallas_call(
        functools.partial(_norm_matmul_body, scale=float(scale), scale_blocks=scale_cols // tn),
        grid=(T // tm, n // tn),
        in_specs=[
            pl.BlockSpec((tm, D), lambda i, j: (i, 0)),
            pl.BlockSpec((1, D), lambda i, j: (0, 0)),
            pl.BlockSpec((None, D, tn), lambda i, j: (layer, 0, j + cb)),
        ],
        out_specs=pl.BlockSpec((tm, tn), lambda i, j: (i, j)),
        out_shape=jax.ShapeDtypeStruct((T, n), out_dtype),
        scratch_shapes=[pltpu.VMEM((tm, D), BF16)],
        compiler_params=_params("parallel", "arbitrary"),
        name="norm_matmul",
    )(x, g.reshape(1, D), w)


def _matmul_residual_body(h_ref, w_ref, x_ref, o_ref):
    o_ref[...] = x_ref[...] + jnp.dot(h_ref[...], w_ref[...].astype(BF16),
                                      preferred_element_type=F32)


def _matmul_residual(h, w, layer, x, *, tm=1024, tn=1024):
    T, K = h.shape
    D = x.shape[1]
    tm, tn = _tile(T, tm), _tile(D, tn)
    return pl.pallas_call(
        _matmul_residual_body,
        grid=(T // tm, D // tn),
        in_specs=[
            pl.BlockSpec((tm, K), lambda i, j: (i, 0)),
            pl.BlockSpec((None, K, tn), lambda i, j: (layer, 0, j)),
            pl.BlockSpec((tm, tn), lambda i, j: (i, j)),
        ],
        out_specs=pl.BlockSpec((tm, tn), lambda i, j: (i, j)),
        out_shape=jax.ShapeDtypeStruct((T, D), F32),
        compiler_params=_params("parallel", "arbitrary"),
        name="matmul_residual",
    )(h, w, x)


def _split3(x):
    h1 = x.astype(BF16)
    r1 = x - h1.astype(F32)
    h2 = r1.astype(BF16)
    h3 = (r1 - h2.astype(F32)).astype(BF16)
    return h1, h2, h3


def _mlstm_body(q_ref, k_ref, v_ref, og_ref, gc_ref, gr_ref, bc_ref, br_ref, hn_ref,
                o_ref, c_ref, n_ref, m_ref, *, L, H, dqk, dv):
    @pl.when(pl.program_id(1) == 0)
    def _():
        c_ref[...] = jnp.zeros(c_ref.shape, F32)
        n_ref[...] = jnp.zeros(n_ref.shape, F32)
        m_ref[...] = jnp.zeros(m_ref.shape, F32)

    row = lax.broadcasted_iota(jnp.int32, (L, L), 0)
    col = lax.broadcasted_iota(jnp.int32, (L, L), 1)
    causal = col <= row
    tri = causal.astype(BF16)

    gcol = gc_ref[...] + bc_ref[...]
    ig_c = GATE_SOFTCAP * jnp.tanh(gcol * (1.0 / GATE_SOFTCAP))
    lf_c = _log_sigmoid(gcol)
    b_c = sum(jnp.dot(tri, p, preferred_element_type=F32) for p in _split3(lf_c))
    grow = gr_ref[...] + br_ref[...]
    ig_r = GATE_SOFTCAP * jnp.tanh(grow * (1.0 / GATE_SOFTCAP))
    lf_r = _log_sigmoid(grow)
    b_r = sum(lax.dot_general(p, tri, (((1,), (1,)), ((), ())), preferred_element_type=F32)
              for p in _split3(lf_r))

    qs = [q_ref[:, h * dqk:(h + 1) * dqk] for h in range(H)]
    ks = [k_ref[:, h * dqk:(h + 1) * dqk] for h in range(H)]
    vs = [v_ref[:, h * dv:(h + 1) * dv] for h in range(H)]
    cts = [c_ref[h] for h in range(H)]
    ss = [lax.dot_general(qs[h], ks[h], (((1,), (1,)), ((), ())), preferred_element_type=F32)
          for h in range(H)]
    qcs = [jnp.dot(qs[h], cts[h].astype(BF16), preferred_element_type=F32) for h in range(H)]

    sws, decs, invs, wvs, cdecs, n_news, m_news = [], [], [], [], [], [], []
    for h in range(H):
        bcol = b_c[:, H + h:H + h + 1]
        icol = ig_c[:, h:h + 1]
        brow = b_r[H + h:H + h + 1, :]
        irow = ig_r[h:h + 1, :]
        m_prev = m_ref[h]
        n_prev = n_ref[h]
        dm = jnp.where(causal, bcol - brow + irow, -jnp.inf)
        m_inter = bcol + m_prev
        m_t = jnp.maximum(jnp.max(dm, axis=1, keepdims=True), m_inter)
        sw = ss[h] * jnp.exp(dm - m_t)
        dec = jnp.exp(m_inter - m_t)
        nq = jnp.sum(qs[h].astype(F32) * n_prev, axis=1, keepdims=True)
        den = jnp.sum(sw, axis=1, keepdims=True) + dec * nq
        invs.append(1.0 / jnp.maximum(jnp.abs(den), jnp.exp(-m_t)))
        sws.append(sw.astype(BF16))
        decs.append(dec)
        b_last = bcol[L - 1:L, :]
        gdec = b_last - bcol + icol
        m_new = jnp.maximum(b_last + m_prev, jnp.max(gdec, axis=0, keepdims=True))
        w = jnp.exp(gdec - m_new)
        cdec = jnp.exp(b_last + m_prev - m_new)
        wvs.append((w * vs[h].astype(F32)).astype(BF16))
        n_news.append(cdec * n_prev + jnp.sum(w * ks[h].astype(F32), axis=0, keepdims=True))
        cdecs.append(cdec)
        m_news.append(m_new)

    nums = [jnp.dot(sws[h], vs[h], preferred_element_type=F32) for h in range(H)]
    kvs = [lax.dot_general(ks[h], wvs[h], (((0,), (0,)), ((), ())), preferred_element_type=F32)
           for h in range(H)]

    for h in range(H):
        hd = slice(h * dv, (h + 1) * dv)
        out = (nums[h] + decs[h] * qcs[h]) * invs[h]
        hs = out * lax.rsqrt(jnp.mean(out * out, axis=1, keepdims=True) + EPS)
        hs = hs * hn_ref[:, hd] * jax.nn.sigmoid(og_ref[:, hd])
        o_ref[:, hd] = hs.astype(o_ref.dtype)
        c_ref[h] = cdecs[h] * cts[h] + kvs[h]
        n_ref[h] = n_news[h]
        m_ref[h] = m_news[h]


def _mlstm_core(qkv, og, gates, b_gate, head_norm, B, S):
    H = A_HEADS
    T = B * S
    dv = og.shape[1] // H
    dqk = dv // 2
    L = _tile(S, MLSTM_CHUNK)
    nc = S // L
    gw = gates.shape[1]
    gates_t = gates[:, :2 * H].T
    bias_c = jnp.pad(b_gate, (0, gw - 2 * H)).reshape(1, gw)
    kern = functools.partial(_mlstm_body, L=L, H=H, dqk=dqk, dv=dv)
    return pl.pallas_call(
        kern,
        grid=(B, nc),
        in_specs=[
            pl.BlockSpec((L, H * dqk), lambda b, c: (b * nc + c, 0)),
            pl.BlockSpec((L, H * dqk), lambda b, c: (b * nc + c, 1)),
            pl.BlockSpec((L, H * dv), lambda b, c: (b * nc + c, 1)),
            pl.BlockSpec((L, H * dv), lambda b, c: (b * nc + c, 0)),
            pl.BlockSpec((L, gw), lambda b, c: (b * nc + c, 0)),
            pl.BlockSpec((2 * H, L), lambda b, c: (0, b * nc + c)),
            pl.BlockSpec((1, gw), lambda b, c: (0, 0)),
            pl.BlockSpec((2 * H, 1), lambda b, c: (0, 0)),
            pl.BlockSpec((1, H * dv), lambda b, c: (0, 0)),
        ],
        out_specs=pl.BlockSpec((L, H * dv), lambda b, c: (b * nc + c, 0)),
        out_shape=jax.ShapeDtypeStruct((T, H * dv), BF16),
        scratch_shapes=[
            pltpu.VMEM((H, dqk, dv), F32),
            pltpu.VMEM((H, 1, dqk), F32),
            pltpu.VMEM((H, 1, 1), F32),
        ],
        compiler_params=_params("parallel", "arbitrary"),
        name="mlstm_core",
    )(qkv, qkv, qkv, og, gates, gates_t, bias_c, b_gate.reshape(2 * H, 1),
      head_norm.reshape(1, H * dv))


def _sb_body(q_ref, k_ref, v_ref, o_ref, *, QB, G, dh):
    qi = pl.program_id(2)
    row = lax.broadcasted_iota(jnp.int32, (QB, QB), 0)
    col = lax.broadcasted_iota(jnp.int32, (QB, QB), 1)
    strict = col < row
    later = (row > col).astype(BF16)
    later2 = jnp.concatenate([later, later], axis=0)

    def step(j, state, diag):
        start = pl.multiple_of(j * QB, QB)
        heads = [slice(g * dh, (g + 1) * dh) for g in range(G)]
        zs = [lax.dot_general(q_ref[:, hs], k_ref[pl.ds(start, QB), hs],
                              (((1,), (1,)), ((), ())), preferred_element_type=F32)
              for hs in heads]
        log_sigs, sps, pieces = [], [], []
        for z in zs:
            neg = jnp.minimum(z, 0.0)
            pos = jnp.maximum(z, 0.0)
            lg = jnp.log(1.0 + jnp.exp(neg - pos))
            sp = pos + lg
            if diag:
                sp = jnp.where(strict, sp, 0.0)
            hi = sp.astype(BF16)
            lo = (sp - hi.astype(F32)).astype(BF16)
            log_sigs.append(neg - lg)
            sps.append(sp)
            pieces.append(jnp.concatenate([hi, lo], axis=1))
        s_afters = [jnp.dot(p, later2, preferred_element_type=F32) for p in pieces]
        atts = []
        for g in range(G):
            att = jnp.exp(log_sigs[g] - (s_afters[g] + state[g][0]))
            if diag:
                att = jnp.where(strict, att, 0.0)
            atts.append(att.astype(BF16))
        new_state = []
        for g in range(G):
            acc = state[g][1] + jnp.dot(atts[g], v_ref[pl.ds(start, QB), heads[g]],
                                        preferred_element_type=F32)
            carry = state[g][0] + jnp.sum(sps[g], axis=1, keepdims=True)
            new_state.append((carry, acc))
        return tuple(new_state)

    state = tuple((jnp.zeros((QB, 1), F32), jnp.zeros((QB, dh), F32)) for _ in range(G))
    state = step(qi, state, True)
    state = lax.fori_loop(0, qi, lambda it, st: step(qi - 1 - it, st, False), state)
    o_ref[...] = jnp.concatenate([st[1] for st in state], axis=1).astype(o_ref.dtype)


def _stick_breaking_core(q, kv, B, S):
    H = B_HEADS
    G = SB_HEADS_PER_STEP
    T = B * S
    dh = q.shape[1] // H
    QB = _tile(S, SB_BLOCK)
    nq = S // QB
    kern = functools.partial(_sb_body, QB=QB, G=G, dh=dh)
    return pl.pallas_call(
        kern,
        grid=(B, H // G, nq),
        in_specs=[
            pl.BlockSpec((QB, G * dh), lambda b, h, i: (b * nq + i, h)),
            pl.BlockSpec((S, G * dh), lambda b, h, i: (b, h)),
            pl.BlockSpec((S, G * dh), lambda b, h, i: (b, H // G + h)),
        ],
        out_specs=pl.BlockSpec((QB, G * dh), lambda b, h, i: (b * nq + i, h)),
        out_shape=jax.ShapeDtypeStruct((T, H * dh), BF16),
        compiler_params=_params("parallel", "parallel", "arbitrary"),
        name="stick_breaking_core",
    )(q, kv, kv)


def kernel(x, ffn1_norm, ffn1_w_in, ffn1_w_out, mix_norm, ffn2_norm, ffn2_w_in, ffn2_w_out,
           a_w_in, a_b_gate, a_head_norm, a_w_out, kv_norm, kv_w, b_w_q, b_w_out, final_norm):
    B, S, D = x.shape
    depth = ffn1_norm.shape[0]
    n_a = a_w_in.shape[0]
    n_qk = A_HEADS * (D // A_HEADS // 2)
    n_qkv = 2 * n_qk + D
    xt = x.reshape(B * S, D)
    kv = None
    for l in range(depth):
        xt = _ffn(xt, ffn1_norm[l], ffn1_w_in, ffn1_w_out, l)
        if l < n_a:
            qkv = _norm_matmul(xt, mix_norm[l], a_w_in, l, 0, n_qkv, BF16,
                               scale=(n_qk // A_HEADS) ** -0.5, scale_cols=n_qk)
            og = _norm_matmul(xt, mix_norm[l], a_w_in, l, n_qkv, D, F32)
            w_gate = jnp.pad(a_w_in[l, :, n_qkv + D:], ((0, 0), (0, LANES - 2 * A_HEADS)))[None]
            gates = _norm_matmul(xt, mix_norm[l], w_gate, 0, 0, LANES, F32)
            hs = _mlstm_core(qkv, og, gates, a_b_gate[l], a_head_norm[l], B, S)
            xt = _matmul_residual(hs, a_w_out, l, xt)
        else:
            j = l - n_a
            q = _norm_matmul(xt, mix_norm[l], b_w_q, j, 0, D, BF16, scale=(D // B_HEADS) ** -0.5)
            o = _stick_breaking_core(q, kv, B, S)
            xt = _matmul_residual(o, b_w_out, j, xt)
        xt = _ffn(xt, ffn2_norm[l], ffn2_w_in, ffn2_w_out, l,
                  out_norm=final_norm if l == depth - 1 else None)
        if l == n_a - 1:
            kv = _norm_matmul(xt, kv_norm, kv_w[None], 0, 0, 2 * D, BF16)
    return xt.reshape(B, S, D)
```

```python
import functools

import jax
import jax.numpy as jnp
from jax import lax
from jax.experimental import pallas as pl
from jax.experimental.pallas import tpu as pltpu

EPS = 1e-6
GATE_SOFTCAP = 15.0
A_HEADS = 8
B_HEADS = 16
MLSTM_CHUNK = 256
SB_BLOCK = 256
SB_HEADS_PER_STEP = 4
LANES = 128
VMEM_LIMIT_BYTES = 56 * 1024 * 1024

F32 = jnp.float32
BF16 = jnp.bfloat16


def _params(*sem):
    return pltpu.CompilerParams(dimension_semantics=sem, vmem_limit_bytes=VMEM_LIMIT_BYTES)


def _tile(n, want):
    t = min(n, want)
    assert n % t == 0, (n, want)
    return t


def _rms_scale(x, g):
    ms = jnp.mean(x * x, axis=-1, keepdims=True)
    return x * lax.rsqrt(ms + EPS) * g


def _log_sigmoid(x):
    return jnp.minimum(x, 0.0) - jnp.log1p(jnp.exp(-jnp.abs(x)))


def _ffn_body(x_ref, g_ref, wg_ref, wu_ref, wo_ref, *rest, has_out_norm):
    if has_out_norm:
        go_ref, o_ref, xn_ref = rest
    else:
        o_ref, xn_ref = rest

    @pl.when(pl.program_id(1) == 0)
    def _():
        x = x_ref[...]
        xn_ref[...] = _rms_scale(x, g_ref[...]).astype(BF16)
        o_ref[...] = x

    xn = xn_ref[...]
    gate = jnp.dot(xn, wg_ref[...].astype(BF16), preferred_element_type=F32)
    up = jnp.dot(xn, wu_ref[...].astype(BF16), preferred_element_type=F32)
    act = (0.5 * gate * jax.nn.sigmoid(gate) * up).astype(BF16)
    o_ref[...] += jnp.dot(act, wo_ref[...].astype(BF16), preferred_element_type=F32)

    if has_out_norm:
        @pl.when(pl.program_id(1) == pl.num_programs(1) - 1)
        def _():
            o_ref[...] = _rms_scale(o_ref[...], go_ref[...])


def _ffn(x, g, w_in, w_out, layer, out_norm=None, *, tm=1024, tf=256):
    T, D = x.shape
    F = w_out.shape[1]
    tm, tf = _tile(T, tm), _tile(F, tf)
    nf = F // tf
    extra_specs, extra_args = [], []
    if out_norm is not None:
        extra_specs = [pl.BlockSpec((1, D), lambda i, j: (0, 0))]
        extra_args = [out_norm.reshape(1, D)]
    return pl.pallas_call(
        functools.partial(_ffn_body, has_out_norm=out_norm is not None),
        grid=(T // tm, nf),
        in_specs=[
            pl.BlockSpec((tm, D), lambda i, j: (i, 0), pipeline_mode=pl.Buffered(1)),
            pl.BlockSpec((1, D), lambda i, j: (0, 0)),
            pl.BlockSpec((None, D, tf), lambda i, j: (layer, 0, j)),
            pl.BlockSpec((None, D, tf), lambda i, j: (layer, 0, j + nf)),
            pl.BlockSpec((None, tf, D), lambda i, j: (layer, j, 0)),
        ] + extra_specs,
        out_specs=pl.BlockSpec((tm, D), lambda i, j: (i, 0)),
        out_shape=jax.ShapeDtypeStruct((T, D), F32),
        scratch_shapes=[pltpu.VMEM((tm, D), BF16)],
        compiler_params=_params("parallel", "arbitrary"),
        name="ffn",
    )(x, g.reshape(1, D), w_in, w_in, w_out, *extra_args)


def _norm_matmul_body(x_ref, g_ref, w_ref, o_ref, xn_ref, *, scale, scale_blocks, w_transposed):
    @pl.when(pl.program_id(1) == 0)
    def _():
        xn_ref[...] = _rms_scale(x_ref[...], g_ref[...]).astype(BF16)

    contract_w = 1 if w_transposed else 0
    y = lax.dot_general(xn_ref[...], w_ref[...].astype(BF16), (((1,), (contract_w,)), ((), ())),
                        preferred_element_type=F32)
    if scale != 1.0:
        y = y * jnp.where(pl.program_id(1) < scale_blocks, scale, 1.0)
    o_ref[...] = y.astype(o_ref.dtype)


def _norm_matmul(x, g, w, layer, col0, n, out_dtype, *, scale=1.0, scale_cols=None,
                 w_transposed=False, tm=1024, tn=1024):
    T, D = x.shape
    tm, tn = _tile(T, tm), _tile(n, tn)
    scale_cols = n if scale_cols is None else scale_cols
    assert col0 % tn == 0 and scale_cols % tn == 0
    cb = col0 // tn
    return pl.pallas_call(
        functools.partial(_norm_matmul_body, scale=float(scale), scale_blocks=scale_cols // tn,
                          w_transposed=w_transposed),
        grid=(T // tm, n // tn),
        in_specs=[
            pl.BlockSpec((tm, D), lambda i, j: (i, 0)),
            pl.BlockSpec((1, D), lambda i, j: (0, 0)),
            (pl.BlockSpec((None, tn, D), lambda i, j: (layer, j + cb, 0)) if w_transposed else
             pl.BlockSpec((None, D, tn), lambda i, j: (layer, 0, j + cb))),
        ],
        out_specs=pl.BlockSpec((tm, tn), lambda i, j: (i, j)),
        out_shape=jax.ShapeDtypeStruct((T, n), out_dtype),
        scratch_shapes=[pltpu.VMEM((tm, D), BF16)],
        compiler_params=_params("parallel", "arbitrary"),
        name="norm_matmul",
    )(x, g.reshape(1, D), w)


def _matmul_residual_body(h_ref, w_ref, x_ref, o_ref):
    o_ref[...] = x_ref[...] + jnp.dot(h_ref[...], w_ref[...].astype(BF16),
                                      preferred_element_type=F32)


def _matmul_residual(h, w, layer, x, *, tm=2048, tn=512):
    T, K = h.shape
    D = x.shape[1]
    tm, tn = _tile(T, tm), _tile(D, tn)
    return pl.pallas_call(
        _matmul_residual_body,
        grid=(T // tm, D // tn),
        in_specs=[
            pl.BlockSpec((tm, K), lambda i, j: (i, 0)),
            pl.BlockSpec((None, K, tn), lambda i, j: (layer, 0, j)),
            pl.BlockSpec((tm, tn), lambda i, j: (i, j)),
        ],
        out_specs=pl.BlockSpec((tm, tn), lambda i, j: (i, j)),
        out_shape=jax.ShapeDtypeStruct((T, D), F32),
        compiler_params=_params("parallel", "arbitrary"),
        name="matmul_residual",
    )(h, w, x)


def _split3(x):
    h1 = x.astype(BF16)
    r1 = x - h1.astype(F32)
    h2 = r1.astype(BF16)
    h3 = (r1 - h2.astype(F32)).astype(BF16)
    return h1, h2, h3


def _mlstm_body(q_ref, k_ref, v_ref, og_ref, gc_ref, gr_ref, bc_ref, br_ref, hn_ref,
                o_ref, c_ref, n_ref, m_ref, *, L, H, dqk, dv):
    @pl.when(pl.program_id(1) == 0)
    def _():
        c_ref[...] = jnp.zeros(c_ref.shape, F32)
        n_ref[...] = jnp.zeros(n_ref.shape, F32)
        m_ref[...] = jnp.zeros(m_ref.shape, F32)

    row = lax.broadcasted_iota(jnp.int32, (L, L), 0)
    col = lax.broadcasted_iota(jnp.int32, (L, L), 1)
    causal = col <= row
    tri = causal.astype(BF16)

    gcol = gc_ref[...] + bc_ref[...]
    ig_c = GATE_SOFTCAP * jnp.tanh(gcol * (1.0 / GATE_SOFTCAP))
    lf_c = _log_sigmoid(gcol)
    b_c = sum(jnp.dot(tri, p, preferred_element_type=F32) for p in _split3(lf_c))
    grow = gr_ref[...] + br_ref[...]
    ig_r = GATE_SOFTCAP * jnp.tanh(grow * (1.0 / GATE_SOFTCAP))
    lf_r = _log_sigmoid(grow)
    b_r = sum(lax.dot_general(p, tri, (((1,), (1,)), ((), ())), preferred_element_type=F32)
              for p in _split3(lf_r))

    qs = [q_ref[:, h * dqk:(h + 1) * dqk] for h in range(H)]
    ks = [k_ref[:, h * dqk:(h + 1) * dqk] for h in range(H)]
    vs = [v_ref[:, h * dv:(h + 1) * dv] for h in range(H)]
    cts = [c_ref[h] for h in range(H)]
    ss = [lax.dot_general(qs[h], ks[h], (((1,), (1,)), ((), ())), preferred_element_type=F32)
          for h in range(H)]
    qcs = [jnp.dot(qs[h], cts[h].astype(BF16), preferred_element_type=F32) for h in range(H)]

    sws, decs, invs, wvs, cdecs, n_news, m_news = [], [], [], [], [], [], []
    for h in range(H):
        bcol = b_c[:, H + h:H + h + 1]
        icol = ig_c[:, h:h + 1]
        brow = b_r[H + h:H + h + 1, :]
        irow = ig_r[h:h + 1, :]
        m_prev = m_ref[h]
        n_prev = n_ref[h]
        dm = jnp.where(causal, bcol - brow + irow, -jnp.inf)
        m_inter = bcol + m_prev
        m_t = jnp.maximum(jnp.max(dm, axis=1, keepdims=True), m_inter)
        sw = ss[h] * jnp.exp(dm - m_t)
        dec = jnp.exp(m_inter - m_t)
        nq = jnp.sum(qs[h].astype(F32) * n_prev, axis=1, keepdims=True)
        den = jnp.sum(sw, axis=1, keepdims=True) + dec * nq
        invs.append(1.0 / jnp.maximum(jnp.abs(den), jnp.exp(-m_t)))
        sws.append(sw.astype(BF16))
        decs.append(dec)
        b_last = bcol[L - 1:L, :]
        gdec = b_last - bcol + icol
        m_new = jnp.maximum(b_last + m_prev, jnp.max(gdec, axis=0, keepdims=True))
        w = jnp.exp(gdec - m_new)
        cdec = jnp.exp(b_last + m_prev - m_new)
        wvs.append((w * vs[h].astype(F32)).astype(BF16))
        n_news.append(cdec * n_prev + jnp.sum(w * ks[h].astype(F32), axis=0, keepdims=True))
        cdecs.append(cdec)
        m_news.append(m_new)

    nums = [jnp.dot(sws[h], vs[h], preferred_element_type=F32) for h in range(H)]
    kvs = [lax.dot_general(ks[h], wvs[h], (((0,), (0,)), ((), ())), preferred_element_type=F32)
           for h in range(H)]

    for h in range(H):
        hd = slice(h * dv, (h + 1) * dv)
        out = (nums[h] + decs[h] * qcs[h]) * invs[h]
        hs = out * lax.rsqrt(jnp.mean(out * out, axis=1, keepdims=True) + EPS)
        hs = hs * hn_ref[:, hd] * jax.nn.sigmoid(og_ref[:, hd])
        o_ref[:, hd] = hs.astype(o_ref.dtype)
        c_ref[h] = cdecs[h] * cts[h] + kvs[h]
        n_ref[h] = n_news[h]
        m_ref[h] = m_news[h]


def _mlstm_core(qkv, og, gates, b_gate, head_norm, B, S):
    H = A_HEADS
    T = B * S
    dv = og.shape[1] // H
    dqk = dv // 2
    L = _tile(S, MLSTM_CHUNK)
    nc = S // L
    gw = gates.shape[1]
    gates_t = gates[:, :2 * H].T
    bias_c = jnp.pad(b_gate, (0, gw - 2 * H)).reshape(1, gw)
    kern = functools.partial(_mlstm_body, L=L, H=H, dqk=dqk, dv=dv)
    return pl.pallas_call(
        kern,
        grid=(B, nc),
        in_specs=[
            pl.BlockSpec((L, H * dqk), lambda b, c: (b * nc + c, 0)),
            pl.BlockSpec((L, H * dqk), lambda b, c: (b * nc + c, 1)),
            pl.BlockSpec((L, H * dv), lambda b, c: (b * nc + c, 1)),
            pl.BlockSpec((L, H * dv), lambda b, c: (b * nc + c, 0)),
            pl.BlockSpec((L, gw), lambda b, c: (b * nc + c, 0)),
            pl.BlockSpec((2 * H, L), lambda b, c: (0, b * nc + c)),
            pl.BlockSpec((1, gw), lambda b, c: (0, 0)),
            pl.BlockSpec((2 * H, 1), lambda b, c: (0, 0)),
            pl.BlockSpec((1, H * dv), lambda b, c: (0, 0)),
        ],
        out_specs=pl.BlockSpec((L, H * dv), lambda b, c: (b * nc + c, 0)),
        out_shape=jax.ShapeDtypeStruct((T, H * dv), BF16),
        scratch_shapes=[
            pltpu.VMEM((H, dqk, dv), F32),
            pltpu.VMEM((H, 1, dqk), F32),
            pltpu.VMEM((H, 1, 1), F32),
        ],
        compiler_params=_params("parallel", "arbitrary"),
        name="mlstm_core",
    )(qkv, qkv, qkv, og, gates, gates_t, bias_c, b_gate.reshape(2 * H, 1),
      head_norm.reshape(1, H * dv))


def _sb_body(q_ref, k_ref, v_ref, o_ref, *, QB, G, dh):
    qi = pl.program_id(2)
    row = lax.broadcasted_iota(jnp.int32, (QB, QB), 0)
    col = lax.broadcasted_iota(jnp.int32, (QB, QB), 1)
    strict = col < row
    later = (row > col).astype(BF16)
    later2 = jnp.concatenate([later, later], axis=0)

    def step(j, state, diag):
        start = pl.multiple_of(j * QB, QB)
        heads = [slice(g * dh, (g + 1) * dh) for g in range(G)]
        zs = [lax.dot_general(q_ref[:, hs], k_ref[pl.ds(start, QB), hs],
                              (((1,), (1,)), ((), ())), preferred_element_type=F32)
              for hs in heads]
        log_sigs, sps, pieces = [], [], []
        for z in zs:
            neg = jnp.minimum(z, 0.0)
            pos = jnp.maximum(z, 0.0)
            lg = jnp.log(1.0 + jnp.exp(neg - pos))
            sp = pos + lg
            if diag:
                sp = jnp.where(strict, sp, 0.0)
            hi = sp.astype(BF16)
            lo = (sp - hi.astype(F32)).astype(BF16)
            log_sigs.append(neg - lg)
            sps.append(sp)
            pieces.append(jnp.concatenate([hi, lo], axis=1))
        s_afters = [jnp.dot(p, later2, preferred_element_type=F32) for p in pieces]
        atts = []
        for g in range(G):
            att = jnp.exp(log_sigs[g] - (s_afters[g] + state[g][0]))
            if diag:
                att = jnp.where(strict, att, 0.0)
            atts.append(att.astype(BF16))
        new_state = []
        for g in range(G):
            acc = state[g][1] + jnp.dot(atts[g], v_ref[pl.ds(start, QB), heads[g]],
                                        preferred_element_type=F32)
            carry = state[g][0] + jnp.sum(sps[g], axis=1, keepdims=True)
            new_state.append((carry, acc))
        return tuple(new_state)

    state = tuple((jnp.zeros((QB, 1), F32), jnp.zeros((QB, dh), F32)) for _ in range(G))
    state = step(qi, state, True)
    state = lax.fori_loop(0, qi, lambda it, st: step(qi - 1 - it, st, False), state)
    o_ref[...] = jnp.concatenate([st[1] for st in state], axis=1).astype(o_ref.dtype)


def _stick_breaking_core(q, kv, B, S):
    H = B_HEADS
    G = SB_HEADS_PER_STEP
    T = B * S
    dh = q.shape[1] // H
    QB = _tile(S, SB_BLOCK)
    nq = S // QB
    kern = functools.partial(_sb_body, QB=QB, G=G, dh=dh)
    return pl.pallas_call(
        kern,
        grid=(B, H // G, nq),
        in_specs=[
            pl.BlockSpec((QB, G * dh), lambda b, h, i: (b * nq + i, h)),
            pl.BlockSpec((S, G * dh), lambda b, h, i: (b, h)),
            pl.BlockSpec((S, G * dh), lambda b, h, i: (b, H // G + h)),
        ],
        out_specs=pl.BlockSpec((QB, G * dh), lambda b, h, i: (b * nq + i, h)),
        out_shape=jax.ShapeDtypeStruct((T, H * dh), BF16),
        compiler_params=_params("parallel", "parallel", "arbitrary"),
        name="stick_breaking_core",
    )(q, kv, kv)


def kernel(x, ffn1_norm, ffn1_w_in, ffn1_w_out, mix_norm, ffn2_norm, ffn2_w_in, ffn2_w_out,
           a_w_in, a_b_gate, a_head_norm, a_w_out, kv_norm, kv_w, b_w_q, b_w_out, final_norm):
    B, S, D = x.shape
    depth = ffn1_norm.shape[0]
    n_a = a_w_in.shape[0]
    n_qk = A_HEADS * (D // A_HEADS // 2)
    n_qkv = 2 * n_qk + D
    xt = x.reshape(B * S, D)
    a_w_in_t = jnp.swapaxes(a_w_in, 1, 2)
    kv = None
    for l in range(depth):
        xt = _ffn(xt, ffn1_norm[l], ffn1_w_in, ffn1_w_out, l)
        if l < n_a:
            qkv = _norm_matmul(xt, mix_norm[l], a_w_in_t, l, 0, n_qkv, BF16, w_transposed=True,
                               scale=(n_qk // A_HEADS) ** -0.5, scale_cols=n_qk)
            og = _norm_matmul(xt, mix_norm[l], a_w_in_t, l, n_qkv, D, F32, w_transposed=True)
            w_gate = jnp.pad(a_w_in_t[l, n_qkv + D:], ((0, LANES - 2 * A_HEADS), (0, 0)))[None]
            gates = _norm_matmul(xt, mix_norm[l], w_gate, 0, 0, LANES, F32, w_transposed=True)
            hs = _mlstm_core(qkv, og, gates, a_b_gate[l], a_head_norm[l], B, S)
            xt = _matmul_residual(hs, a_w_out, l, xt)
        else:
            j = l - n_a
            q = _norm_matmul(xt, mix_norm[l], b_w_q, j, 0, D, BF16, scale=(D // B_HEADS) ** -0.5)
            o = _stick_breaking_core(q, kv, B, S)
            xt = _matmul_residual(o, b_w_out, j, xt)
        xt = _ffn(xt, ffn2_norm[l], ffn2_w_in, ffn2_w_out, l,
                  out_norm=final_norm if l == depth - 1 else None)
        if l == n_a - 1:
            kv = _norm_matmul(xt, kv_norm, kv_w[None], 0, 0, 2 * D, BF16)
    return xt.reshape(B, S, D)
```

```python
import functools

import jax
import jax.numpy as jnp
from jax import lax
from jax.experimental import pallas as pl
from jax.experimental.pallas import tpu as pltpu

EPS = 1e-6
GATE_SOFTCAP = 15.0
A_HEADS = 8
B_HEADS = 16
MLSTM_CHUNK = 256
SB_BLOCK = 256
SB_HEADS_PER_STEP = 4
LANES = 128
VMEM_LIMIT_BYTES = 56 * 1024 * 1024

F32 = jnp.float32
BF16 = jnp.bfloat16


def _params(*sem):
    return pltpu.CompilerParams(dimension_semantics=sem, vmem_limit_bytes=VMEM_LIMIT_BYTES)


def _tile(n, want):
    t = min(n, want)
    assert n % t == 0, (n, want)
    return t


def _rms_scale(x, g):
    ms = jnp.mean(x * x, axis=-1, keepdims=True)
    return x * lax.rsqrt(ms + EPS) * g


def _log_sigmoid(x):
    return jnp.minimum(x, 0.0) - jnp.log1p(jnp.exp(-jnp.abs(x)))


def _ffn_body(x_ref, g_ref, wg_ref, wu_ref, wo_ref, *rest, has_out_norm):
    if has_out_norm:
        go_ref, o_ref, xn_ref = rest
    else:
        o_ref, xn_ref = rest

    def half_swiglu(xn):
        gate = jnp.dot(xn, wg_ref[...].astype(BF16), preferred_element_type=F32)
        up = jnp.dot(xn, wu_ref[...].astype(BF16), preferred_element_type=F32)
        act = (0.5 * gate * jax.nn.sigmoid(gate) * up).astype(BF16)
        return jnp.dot(act, wo_ref[...].astype(BF16), preferred_element_type=F32)

    @pl.when(pl.program_id(1) == 0)
    def _():
        x = x_ref[...]
        xn = _rms_scale(x, g_ref[...]).astype(BF16)
        xn_ref[...] = xn
        o_ref[...] = x + half_swiglu(xn)

    @pl.when(pl.program_id(1) != 0)
    def _():
        o_ref[...] += half_swiglu(xn_ref[...])

    if has_out_norm:
        @pl.when(pl.program_id(1) == pl.num_programs(1) - 1)
        def _():
            o_ref[...] = _rms_scale(o_ref[...], go_ref[...])


def _ffn(x, g, w_in, w_out, layer, out_norm=None, *, tm=1024, tf=256):
    T, D = x.shape
    F = w_out.shape[1]
    tm, tf = _tile(T, tm), _tile(F, tf)
    nf = F // tf
    extra_specs, extra_args = [], []
    if out_norm is not None:
        extra_specs = [pl.BlockSpec((1, D), lambda i, j: (0, 0))]
        extra_args = [out_norm.reshape(1, D)]
    return pl.pallas_call(
        functools.partial(_ffn_body, has_out_norm=out_norm is not None),
        grid=(T // tm, nf),
        in_specs=[
            pl.BlockSpec((tm, D), lambda i, j: (i, 0), pipeline_mode=pl.Buffered(1)),
            pl.BlockSpec((1, D), lambda i, j: (0, 0)),
            pl.BlockSpec((None, D, tf), lambda i, j: (layer, 0, j)),
            pl.BlockSpec((None, D, tf), lambda i, j: (layer, 0, j + nf)),
            pl.BlockSpec((None, tf, D), lambda i, j: (layer, j, 0)),
        ] + extra_specs,
        out_specs=pl.BlockSpec((tm, D), lambda i, j: (i, 0)),
        out_shape=jax.ShapeDtypeStruct((T, D), F32),
        scratch_shapes=[pltpu.VMEM((tm, D), BF16)],
        compiler_params=_params("parallel", "arbitrary"),
        name="ffn",
    )(x, g.reshape(1, D), w_in, w_in, w_out, *extra_args)


def _norm_matmul_body(x_ref, g_ref, w_ref, o_ref, xn_ref, *, scale, scale_blocks, w_transposed):
    def project(xn):
        contract_w = 1 if w_transposed else 0
        y = lax.dot_general(xn, w_ref[...].astype(BF16), (((1,), (contract_w,)), ((), ())),
                            preferred_element_type=F32)
        if scale != 1.0:
            y = y * jnp.where(pl.program_id(1) < scale_blocks, scale, 1.0)
        o_ref[...] = y.astype(o_ref.dtype)

    @pl.when(pl.program_id(1) == 0)
    def _():
        xn = _rms_scale(x_ref[...], g_ref[...]).astype(BF16)
        xn_ref[...] = xn
        project(xn)

    @pl.when(pl.program_id(1) != 0)
    def _():
        project(xn_ref[...])


def _norm_matmul(x, g, w, layer, col0, n, out_dtype, *, scale=1.0, scale_cols=None,
                 w_transposed=False, tm=1024, tn=1024):
    T, D = x.shape
    tm, tn = _tile(T, tm), _tile(n, tn)
    scale_cols = n if scale_cols is None else scale_cols
    assert col0 % tn == 0 and scale_cols % tn == 0
    cb = col0 // tn
    return pl.pallas_call(
        functools.partial(_norm_matmul_body, scale=float(scale), scale_blocks=scale_cols // tn,
                          w_transposed=w_transposed),
        grid=(T // tm, n // tn),
        in_specs=[
            pl.BlockSpec((tm, D), lambda i, j: (i, 0)),
            pl.BlockSpec((1, D), lambda i, j: (0, 0)),
            (pl.BlockSpec((None, tn, D), lambda i, j: (layer, j + cb, 0)) if w_transposed else
             pl.BlockSpec((None, D, tn), lambda i, j: (layer, 0, j + cb))),
        ],
        out_specs=pl.BlockSpec((tm, tn), lambda i, j: (i, j)),
        out_shape=jax.ShapeDtypeStruct((T, n), out_dtype),
        scratch_shapes=[pltpu.VMEM((tm, D), BF16)],
        compiler_params=_params("parallel", "arbitrary"),
        name="norm_matmul",
    )(x, g.reshape(1, D), w)


def _matmul_residual_body(h_ref, w_ref, x_ref, o_ref):
    o_ref[...] = x_ref[...] + jnp.dot(h_ref[...], w_ref[...].astype(BF16),
                                      preferred_element_type=F32)


def _matmul_residual(h, w, layer, x, *, tm=2048, tn=512):
    T, K = h.shape
    D = x.shape[1]
    tm, tn = _tile(T, tm), _tile(D, tn)
    return pl.pallas_call(
        _matmul_residual_body,
        grid=(T // tm, D // tn),
        in_specs=[
            pl.BlockSpec((tm, K), lambda i, j: (i, 0)),
            pl.BlockSpec((None, K, tn), lambda i, j: (layer, 0, j)),
            pl.BlockSpec((tm, tn), lambda i, j: (i, j)),
        ],
        out_specs=pl.BlockSpec((tm, tn), lambda i, j: (i, j)),
        out_shape=jax.ShapeDtypeStruct((T, D), F32),
        compiler_params=_params("parallel", "arbitrary"),
        name="matmul_residual",
    )(h, w, x)


def _split3(x):
    h1 = x.astype(BF16)
    r1 = x - h1.astype(F32)
    h2 = r1.astype(BF16)
    h3 = (r1 - h2.astype(F32)).astype(BF16)
    return h1, h2, h3


def _mlstm_body(q_ref, k_ref, v_ref, og_ref, gc_ref, gr_ref, bc_ref, br_ref, hn_ref,
                o_ref, c_ref, n_ref, m_ref, *, L, H, dqk, dv):
    @pl.when(pl.program_id(1) == 0)
    def _():
        c_ref[...] = jnp.zeros(c_ref.shape, F32)
        n_ref[...] = jnp.zeros(n_ref.shape, F32)
        m_ref[...] = jnp.zeros(m_ref.shape, F32)

    row = lax.broadcasted_iota(jnp.int32, (L, L), 0)
    col = lax.broadcasted_iota(jnp.int32, (L, L), 1)
    causal = col <= row
    tri = causal.astype(BF16)

    gcol = gc_ref[...] + bc_ref[...]
    ig_c = GATE_SOFTCAP * jnp.tanh(gcol * (1.0 / GATE_SOFTCAP))
    lf_c = _log_sigmoid(gcol)
    b_c = sum(jnp.dot(tri, p, preferred_element_type=F32) for p in _split3(lf_c))
    grow = gr_ref[...] + br_ref[...]
    ig_r = GATE_SOFTCAP * jnp.tanh(grow * (1.0 / GATE_SOFTCAP))
    lf_r = _log_sigmoid(grow)
    b_r = sum(lax.dot_general(p, tri, (((1,), (1,)), ((), ())), preferred_element_type=F32)
              for p in _split3(lf_r))

    qs = [q_ref[:, h * dqk:(h + 1) * dqk] for h in range(H)]
    ks = [k_ref[:, h * dqk:(h + 1) * dqk] for h in range(H)]
    vs = [v_ref[:, h * dv:(h + 1) * dv] for h in range(H)]
    cts = [c_ref[h] for h in range(H)]
    ss = [lax.dot_general(qs[h], ks[h], (((1,), (1,)), ((), ())), preferred_element_type=F32)
          for h in range(H)]
    qcs = [jnp.dot(qs[h], cts[h].astype(BF16), preferred_element_type=F32) for h in range(H)]

    sws, decs, invs, wvs, cdecs, n_news, m_news = [], [], [], [], [], [], []
    for h in range(H):
        bcol = b_c[:, H + h:H + h + 1]
        icol = ig_c[:, h:h + 1]
        brow = b_r[H + h:H + h + 1, :]
        irow = ig_r[h:h + 1, :]
        m_prev = m_ref[h]
        n_prev = n_ref[h]
        dm = jnp.where(causal, bcol - brow + irow, -jnp.inf)
        m_inter = bcol + m_prev
        m_t = jnp.maximum(jnp.max(dm, axis=1, keepdims=True), m_inter)
        sw = ss[h] * jnp.exp(dm - m_t)
        dec = jnp.exp(m_inter - m_t)
        nq = jnp.sum(qs[h].astype(F32) * n_prev, axis=1, keepdims=True)
        den = jnp.sum(sw, axis=1, keepdims=True) + dec * nq
        invs.append(1.0 / jnp.maximum(jnp.abs(den), jnp.exp(-m_t)))
        sws.append(sw.astype(BF16))
        decs.append(dec)
        b_last = bcol[L - 1:L, :]
        gdec = b_last - bcol + icol
        m_new = jnp.maximum(b_last + m_prev, jnp.max(gdec, axis=0, keepdims=True))
        w = jnp.exp(gdec - m_new)
        cdec = jnp.exp(b_last + m_prev - m_new)
        wvs.append((w * vs[h].astype(F32)).astype(BF16))
        n_news.append(cdec * n_prev + jnp.sum(w * ks[h].astype(F32), axis=0, keepdims=True))
        cdecs.append(cdec)
        m_news.append(m_new)

    nums = [jnp.dot(sws[h], vs[h], preferred_element_type=F32) for h in range(H)]
    kvs = [lax.dot_general(ks[h], wvs[h], (((0,), (0,)), ((), ())), preferred_element_type=F32)
           for h in range(H)]

    for h in range(H):
        hd = slice(h * dv, (h + 1) * dv)
        out = (nums[h] + decs[h] * qcs[h]) * invs[h]
        hs = out * lax.rsqrt(jnp.mean(out * out, axis=1, keepdims=True) + EPS)
        hs = hs * hn_ref[:, hd] * jax.nn.sigmoid(og_ref[:, hd])
        o_ref[:, hd] = hs.astype(o_ref.dtype)
        c_ref[h] = cdecs[h] * cts[h] + kvs[h]
        n_ref[h] = n_news[h]
        m_ref[h] = m_news[h]


def _mlstm_core(qkv, og, gates, b_gate, head_norm, B, S):
    H = A_HEADS
    T = B * S
    dv = og.shape[1] // H
    dqk = dv // 2
    L = _tile(S, MLSTM_CHUNK)
    nc = S // L
    gw = gates.shape[1]
    gates_t = gates[:, :2 * H].T
    bias_c = jnp.pad(b_gate, (0, gw - 2 * H)).reshape(1, gw)
    kern = functools.partial(_mlstm_body, L=L, H=H, dqk=dqk, dv=dv)
    return pl.pallas_call(
        kern,
        grid=(B, nc),
        in_specs=[
            pl.BlockSpec((L, H * dqk), lambda b, c: (b * nc + c, 0)),
            pl.BlockSpec((L, H * dqk), lambda b, c: (b * nc + c, 1)),
            pl.BlockSpec((L, H * dv), lambda b, c: (b * nc + c, 1)),
            pl.BlockSpec((L, H * dv), lambda b, c: (b * nc + c, 0)),
            pl.BlockSpec((L, gw), lambda b, c: (b * nc + c, 0)),
            pl.BlockSpec((2 * H, L), lambda b, c: (0, b * nc + c)),
            pl.BlockSpec((1, gw), lambda b, c: (0, 0)),
            pl.BlockSpec((2 * H, 1), lambda b, c: (0, 0)),
            pl.BlockSpec((1, H * dv), lambda b, c: (0, 0)),
        ],
        out_specs=pl.BlockSpec((L, H * dv), lambda b, c: (b * nc + c, 0)),
        out_shape=jax.ShapeDtypeStruct((T, H * dv), BF16),
        scratch_shapes=[
            pltpu.VMEM((H, dqk, dv), F32),
            pltpu.VMEM((H, 1, dqk), F32),
            pltpu.VMEM((H, 1, 1), F32),
        ],
        compiler_params=_params("parallel", "arbitrary"),
        name="mlstm_core",
    )(qkv, qkv, qkv, og, gates, gates_t, bias_c, b_gate.reshape(2 * H, 1),
      head_norm.reshape(1, H * dv))


def _sb_body(q_ref, k_ref, v_ref, o_ref, *, QB, G, dh):
    qi = pl.program_id(2)
    row = lax.broadcasted_iota(jnp.int32, (QB, QB), 0)
    col = lax.broadcasted_iota(jnp.int32, (QB, QB), 1)
    strict = col < row
    later = (row > col).astype(BF16)
    later2 = jnp.concatenate([later, later], axis=0)

    def step(j, state, diag):
        start = pl.multiple_of(j * QB, QB)
        heads = [slice(g * dh, (g + 1) * dh) for g in range(G)]
        zs = [lax.dot_general(q_ref[:, hs], k_ref[pl.ds(start, QB), hs],
                              (((1,), (1,)), ((), ())), preferred_element_type=F32)
              for hs in heads]
        log_sigs, carries, pieces = [], [], []
        for g, z in enumerate(zs):
            neg = jnp.minimum(z, 0.0)
            pos = jnp.maximum(z, 0.0)
            lg = jnp.log(1.0 + jnp.exp(neg - pos))
            sp = pos + lg
            if diag:
                sp = jnp.where(strict, sp, 0.0)
            hi = sp.astype(BF16)
            lo = (sp - hi.astype(F32)).astype(BF16)
            log_sigs.append((neg - lg) - state[g][0])
            carries.append(state[g][0] + jnp.sum(sp, axis=1, keepdims=True))
            pieces.append(jnp.concatenate([hi, lo], axis=1))
        s_afters = [jnp.dot(p, later2, preferred_element_type=F32) for p in pieces]
        atts = []
        for g in range(G):
            att = jnp.exp(log_sigs[g] - s_afters[g])
            if diag:
                att = jnp.where(strict, att, 0.0)
            atts.append(att.astype(BF16))
        new_state = []
        for g in range(G):
            acc = state[g][1] + jnp.dot(atts[g], v_ref[pl.ds(start, QB), heads[g]],
                                        preferred_element_type=F32)
            new_state.append((carries[g], acc))
        return tuple(new_state)

    state = tuple((jnp.zeros((QB, 1), F32), jnp.zeros((QB, dh), F32)) for _ in range(G))
    state = step(qi, state, True)
    state = lax.fori_loop(0, qi, lambda it, st: step(qi - 1 - it, st, False), state)
    o_ref[...] = jnp.concatenate([st[1] for st in state], axis=1).astype(o_ref.dtype)


def _stick_breaking_core(q, kv, B, S):
    H = B_HEADS
    G = SB_HEADS_PER_STEP
    T = B * S
    dh = q.shape[1] // H
    QB = _tile(S, SB_BLOCK)
    nq = S // QB
    kern = functools.partial(_sb_body, QB=QB, G=G, dh=dh)
    return pl.pallas_call(
        kern,
        grid=(B, H // G, nq),
        in_specs=[
            pl.BlockSpec((QB, G * dh), lambda b, h, i: (b * nq + i, h)),
            pl.BlockSpec((S, G * dh), lambda b, h, i: (b, h)),
            pl.BlockSpec((S, G * dh), lambda b, h, i: (b, H // G + h)),
        ],
        out_specs=pl.BlockSpec((QB, G * dh), lambda b, h, i: (b * nq + i, h)),
        out_shape=jax.ShapeDtypeStruct((T, H * dh), BF16),
        compiler_params=_params("parallel", "parallel", "arbitrary"),
        name="stick_breaking_core",
    )(q, kv, kv)


def kernel(x, ffn1_norm, ffn1_w_in, ffn1_w_out, mix_norm, ffn2_norm, ffn2_w_in, ffn2_w_out,
           a_w_in, a_b_gate, a_head_norm, a_w_out, kv_norm, kv_w, b_w_q, b_w_out, final_norm):
    B, S, D = x.shape
    depth = ffn1_norm.shape[0]
    n_a = a_w_in.shape[0]
    n_qk = A_HEADS * (D // A_HEADS // 2)
    n_qkv = 2 * n_qk + D
    xt = x.reshape(B * S, D)
    a_w_in_t = jnp.swapaxes(a_w_in, 1, 2)
    kv = None
    for l in range(depth):
        xt = _ffn(xt, ffn1_norm[l], ffn1_w_in, ffn1_w_out, l)
        if l < n_a:
            qkv = _norm_matmul(xt, mix_norm[l], a_w_in_t, l, 0, n_qkv, BF16, w_transposed=True,
                               scale=(n_qk // A_HEADS) ** -0.5, scale_cols=n_qk)
            og = _norm_matmul(xt, mix_norm[l], a_w_in_t, l, n_qkv, D, F32, w_transposed=True)
            w_gate = jnp.pad(a_w_in_t[l, n_qkv + D:], ((0, LANES - 2 * A_HEADS), (0, 0)))[None]
            gates = _norm_matmul(xt, mix_norm[l], w_gate, 0, 0, LANES, F32, w_transposed=True)
            hs = _mlstm_core(qkv, og, gates, a_b_gate[l], a_head_norm[l], B, S)
            xt = _matmul_residual(hs, a_w_out, l, xt)
        else:
            j = l - n_a
            q = _norm_matmul(xt, mix_norm[l], b_w_q, j, 0, D, BF16, scale=(D // B_HEADS) ** -0.5)
            o = _stick_breaking_core(q, kv, B, S)
            xt = _matmul_residual(o, b_w_out, j, xt)
        xt = _ffn(xt, ffn2_norm[l], ffn2_w_in, ffn2_w_out, l,
                  out_norm=final_norm if l == depth - 1 else None)
        if l == n_a - 1:
            kv = _norm_matmul(xt, kv_norm, kv_w[None], 0, 0, 2 * D, BF16)
    return xt.reshape(B, S, D)
```

```python
import functools

import jax
import jax.numpy as jnp
from jax import lax
from jax.experimental import pallas as pl
from jax.experimental.pallas import tpu as pltpu

EPS = 1e-6
GATE_SOFTCAP = 15.0
A_HEADS = 8
B_HEADS = 16
MLSTM_CHUNK = 256
SB_BLOCK = 256
SB_HEADS_PER_STEP = 4
LANES = 128
LOG2E = 1.4426950408889634
VMEM_LIMIT_BYTES = 56 * 1024 * 1024

F32 = jnp.float32
BF16 = jnp.bfloat16


def _params(*sem):
    return pltpu.CompilerParams(dimension_semantics=sem, vmem_limit_bytes=VMEM_LIMIT_BYTES)


def _tile(n, want):
    t = min(n, want)
    assert n % t == 0, (n, want)
    return t


def _rms_scale(x, g):
    ms = jnp.mean(x * x, axis=-1, keepdims=True)
    return x * lax.rsqrt(ms + EPS) * g


def _log_sigmoid(x):
    return jnp.minimum(x, 0.0) - jnp.log1p(jnp.exp(-jnp.abs(x)))


def _ffn_body(x_ref, g_ref, wg_ref, wu_ref, wo_ref, *rest, has_out_norm):
    if has_out_norm:
        go_ref, o_ref, xn_ref = rest
    else:
        o_ref, xn_ref = rest

    def half_swiglu(xn):
        gate = jnp.dot(xn, wg_ref[...].astype(BF16), preferred_element_type=F32)
        up = jnp.dot(xn, wu_ref[...].astype(BF16), preferred_element_type=F32)
        act = (0.5 * gate * jax.nn.sigmoid(gate) * up).astype(BF16)
        return jnp.dot(act, wo_ref[...].astype(BF16), preferred_element_type=F32)

    @pl.when(pl.program_id(1) == 0)
    def _():
        x = x_ref[...]
        xn = _rms_scale(x, g_ref[...]).astype(BF16)
        xn_ref[...] = xn
        o_ref[...] = x + half_swiglu(xn)

    @pl.when(pl.program_id(1) != 0)
    def _():
        o_ref[...] += half_swiglu(xn_ref[...])

    if has_out_norm:
        @pl.when(pl.program_id(1) == pl.num_programs(1) - 1)
        def _():
            o_ref[...] = _rms_scale(o_ref[...], go_ref[...])


def _ffn(x, g, w_in, w_out, layer, out_norm=None, *, tm=1024, tf=256):
    T, D = x.shape
    F = w_out.shape[1]
    tm, tf = _tile(T, tm), _tile(F, tf)
    nf = F // tf
    extra_specs, extra_args = [], []
    if out_norm is not None:
        extra_specs = [pl.BlockSpec((1, D), lambda i, j: (0, 0))]
        extra_args = [out_norm.reshape(1, D)]
    return pl.pallas_call(
        functools.partial(_ffn_body, has_out_norm=out_norm is not None),
        grid=(T // tm, nf),
        in_specs=[
            pl.BlockSpec((tm, D), lambda i, j: (i, 0), pipeline_mode=pl.Buffered(1)),
            pl.BlockSpec((1, D), lambda i, j: (0, 0)),
            pl.BlockSpec((None, D, tf), lambda i, j: (layer, 0, j)),
            pl.BlockSpec((None, D, tf), lambda i, j: (layer, 0, j + nf)),
            pl.BlockSpec((None, tf, D), lambda i, j: (layer, j, 0)),
        ] + extra_specs,
        out_specs=pl.BlockSpec((tm, D), lambda i, j: (i, 0)),
        out_shape=jax.ShapeDtypeStruct((T, D), F32),
        scratch_shapes=[pltpu.VMEM((tm, D), BF16)],
        compiler_params=_params("parallel", "arbitrary"),
        name="ffn",
    )(x, g.reshape(1, D), w_in, w_in, w_out, *extra_args)


def _norm_matmul_body(x_ref, g_ref, w_ref, o_ref, xn_ref, *, scale, scale_blocks, w_transposed):
    def project(xn):
        contract_w = 1 if w_transposed else 0
        y = lax.dot_general(xn, w_ref[...].astype(BF16), (((1,), (contract_w,)), ((), ())),
                            preferred_element_type=F32)
        if scale != 1.0:
            y = y * jnp.where(pl.program_id(1) < scale_blocks, scale, 1.0)
        o_ref[...] = y.astype(o_ref.dtype)

    @pl.when(pl.program_id(1) == 0)
    def _():
        xn = _rms_scale(x_ref[...], g_ref[...]).astype(BF16)
        xn_ref[...] = xn
        project(xn)

    @pl.when(pl.program_id(1) != 0)
    def _():
        project(xn_ref[...])


def _norm_matmul(x, g, w, layer, col0, n, out_dtype, *, scale=1.0, scale_cols=None,
                 w_transposed=False, tm=1024, tn=1024):
    T, D = x.shape
    tm, tn = _tile(T, tm), _tile(n, tn)
    scale_cols = n if scale_cols is None else scale_cols
    assert col0 % tn == 0 and scale_cols % tn == 0
    cb = col0 // tn
    return pl.pallas_call(
        functools.partial(_norm_matmul_body, scale=float(scale), scale_blocks=scale_cols // tn,
                          w_transposed=w_transposed),
        grid=(T // tm, n // tn),
        in_specs=[
            pl.BlockSpec((tm, D), lambda i, j: (i, 0)),
            pl.BlockSpec((1, D), lambda i, j: (0, 0)),
            (pl.BlockSpec((None, tn, D), lambda i, j: (layer, j + cb, 0)) if w_transposed else
             pl.BlockSpec((None, D, tn), lambda i, j: (layer, 0, j + cb))),
        ],
        out_specs=pl.BlockSpec((tm, tn), lambda i, j: (i, j)),
        out_shape=jax.ShapeDtypeStruct((T, n), out_dtype),
        scratch_shapes=[pltpu.VMEM((tm, D), BF16)],
        compiler_params=_params("parallel", "arbitrary"),
        name="norm_matmul",
    )(x, g.reshape(1, D), w)


def _mlstm_proj_body(x_ref, g_ref, w_ref, wgate_ref, qkv_ref, og_ref, gates_ref, xn_ref, *,
                     scale, scale_blocks, qkv_blocks):
    nt = (((1,), (1,)), ((), ()))
    j = pl.program_id(1)

    def project(xn):
        return lax.dot_general(xn, w_ref[...].astype(BF16), nt, preferred_element_type=F32)

    @pl.when(j == 0)
    def _():
        xn = _rms_scale(x_ref[...], g_ref[...]).astype(BF16)
        xn_ref[...] = xn
        gates_ref[...] = lax.dot_general(xn, wgate_ref[...].astype(BF16), nt,
                                         preferred_element_type=F32)
        qkv_ref[...] = (project(xn) * scale).astype(qkv_ref.dtype)

    @pl.when((j != 0) & (j < qkv_blocks))
    def _():
        y = project(xn_ref[...]) * jnp.where(j < scale_blocks, scale, 1.0)
        qkv_ref[...] = y.astype(qkv_ref.dtype)

    @pl.when(j >= qkv_blocks)
    def _():
        og_ref[...] = project(xn_ref[...])


def _mlstm_proj(x, g, w_t, layer, w_gate, n_q, n_qkv, n_og, *, scale, tm=1024, tn=512):
    T, D = x.shape
    tm = _tile(T, tm)
    assert n_q % tn == 0 and n_qkv % tn == 0 and n_og % tn == 0 and n_q >= tn
    qb, ob = n_qkv // tn, n_og // tn
    return pl.pallas_call(
        functools.partial(_mlstm_proj_body, scale=float(scale), scale_blocks=n_q // tn,
                          qkv_blocks=qb),
        grid=(T // tm, qb + ob),
        in_specs=[
            pl.BlockSpec((tm, D), lambda i, j: (i, 0)),
            pl.BlockSpec((1, D), lambda i, j: (0, 0)),
            pl.BlockSpec((None, tn, D), lambda i, j: (layer, j, 0)),
            pl.BlockSpec((LANES, D), lambda i, j: (0, 0)),
        ],
        out_specs=[
            pl.BlockSpec((tm, tn), lambda i, j: (i, jnp.minimum(j, qb - 1))),
            pl.BlockSpec((tm, tn), lambda i, j: (i, jnp.maximum(j - qb, 0))),
            pl.BlockSpec((tm, LANES), lambda i, j: (i, 0)),
        ],
        out_shape=[jax.ShapeDtypeStruct((T, n_qkv), BF16),
                   jax.ShapeDtypeStruct((T, n_og), F32),
                   jax.ShapeDtypeStruct((T, LANES), F32)],
        scratch_shapes=[pltpu.VMEM((tm, D), BF16)],
        compiler_params=_params("parallel", "arbitrary"),
        name="mlstm_proj",
    )(x, g.reshape(1, D), w_t, w_gate)


def _matmul_residual_body(h_ref, w_ref, x_ref, o_ref):
    o_ref[...] = x_ref[...] + jnp.dot(h_ref[...], w_ref[...].astype(BF16),
                                      preferred_element_type=F32)


def _matmul_residual(h, w, layer, x, *, tm=2048, tn=512):
    T, K = h.shape
    D = x.shape[1]
    tm, tn = _tile(T, tm), _tile(D, tn)
    return pl.pallas_call(
        _matmul_residual_body,
        grid=(T // tm, D // tn),
        in_specs=[
            pl.BlockSpec((tm, K), lambda i, j: (i, 0)),
            pl.BlockSpec((None, K, tn), lambda i, j: (layer, 0, j)),
            pl.BlockSpec((tm, tn), lambda i, j: (i, j)),
        ],
        out_specs=pl.BlockSpec((tm, tn), lambda i, j: (i, j)),
        out_shape=jax.ShapeDtypeStruct((T, D), F32),
        compiler_params=_params("parallel", "arbitrary"),
        name="matmul_residual",
    )(h, w, x)


def _split3(x):
    h1 = x.astype(BF16)
    r1 = x - h1.astype(F32)
    h2 = r1.astype(BF16)
    h3 = (r1 - h2.astype(F32)).astype(BF16)
    return h1, h2, h3


def _mlstm_body(q_ref, k_ref, v_ref, og_ref, gc_ref, gr_ref, bc_ref, br_ref, hn_ref,
                o_ref, c_ref, n_ref, m_ref, *, L, H, dqk, dv):
    @pl.when(pl.program_id(1) == 0)
    def _():
        c_ref[...] = jnp.zeros(c_ref.shape, F32)
        n_ref[...] = jnp.zeros(n_ref.shape, F32)
        m_ref[...] = jnp.zeros(m_ref.shape, F32)

    row = lax.broadcasted_iota(jnp.int32, (L, L), 0)
    col = lax.broadcasted_iota(jnp.int32, (L, L), 1)
    causal = col <= row
    tri = causal.astype(BF16)

    gcol = gc_ref[...] + bc_ref[...]
    ig_c = GATE_SOFTCAP * jnp.tanh(gcol * (1.0 / GATE_SOFTCAP))
    lf_c = _log_sigmoid(gcol)
    b_c = sum(jnp.dot(tri, p, preferred_element_type=F32) for p in _split3(lf_c))
    grow = gr_ref[...] + br_ref[...]
    ig_r = GATE_SOFTCAP * jnp.tanh(grow * (1.0 / GATE_SOFTCAP))
    lf_r = _log_sigmoid(grow)
    b_r = sum(lax.dot_general(p, tri, (((1,), (1,)), ((), ())), preferred_element_type=F32)
              for p in _split3(lf_r))

    qs = [q_ref[:, h * dqk:(h + 1) * dqk] for h in range(H)]
    ks = [k_ref[:, h * dqk:(h + 1) * dqk] for h in range(H)]
    vs = [v_ref[:, h * dv:(h + 1) * dv] for h in range(H)]
    cts = [c_ref[h] for h in range(H)]
    ss = [lax.dot_general(qs[h], ks[h], (((1,), (1,)), ((), ())), preferred_element_type=F32)
          for h in range(H)]
    qcs = [jnp.dot(qs[h], cts[h].astype(BF16), preferred_element_type=F32) for h in range(H)]

    sws, decs, invs, wvs, cdecs, n_news, m_news = [], [], [], [], [], [], []
    for h in range(H):
        bcol = b_c[:, H + h:H + h + 1]
        icol = ig_c[:, h:h + 1]
        brow = b_r[H + h:H + h + 1, :]
        irow = ig_r[h:h + 1, :]
        m_prev = m_ref[h]
        n_prev = n_ref[h]
        dm = jnp.where(causal, bcol - brow + irow, -jnp.inf)
        m_inter = bcol + m_prev
        m_t = jnp.maximum(jnp.max(dm, axis=1, keepdims=True), m_inter)
        sw = ss[h] * jnp.exp(dm - m_t)
        dec = jnp.exp(m_inter - m_t)
        nq = jnp.sum(qs[h].astype(F32) * n_prev, axis=1, keepdims=True)
        den = jnp.sum(sw, axis=1, keepdims=True) + dec * nq
        invs.append(1.0 / jnp.maximum(jnp.abs(den), jnp.exp(-m_t)))
        sws.append(sw.astype(BF16))
        decs.append(dec)
        b_last = bcol[L - 1:L, :]
        gdec = b_last - bcol + icol
        m_new = jnp.maximum(b_last + m_prev, jnp.max(gdec, axis=0, keepdims=True))
        w = jnp.exp(gdec - m_new)
        cdec = jnp.exp(b_last + m_prev - m_new)
        wvs.append((w * vs[h].astype(F32)).astype(BF16))
        n_news.append(cdec * n_prev + jnp.sum(w * ks[h].astype(F32), axis=0, keepdims=True))
        cdecs.append(cdec)
        m_news.append(m_new)

    nums = [jnp.dot(sws[h], vs[h], preferred_element_type=F32) for h in range(H)]
    kvs = [lax.dot_general(ks[h], wvs[h], (((0,), (0,)), ((), ())), preferred_element_type=F32)
           for h in range(H)]

    for h in range(H):
        hd = slice(h * dv, (h + 1) * dv)
        out = (nums[h] + decs[h] * qcs[h]) * invs[h]
        hs = out * lax.rsqrt(jnp.mean(out * out, axis=1, keepdims=True) + EPS)
        hs = hs * hn_ref[:, hd] * jax.nn.sigmoid(og_ref[:, hd])
        o_ref[:, hd] = hs.astype(o_ref.dtype)
        c_ref[h] = cdecs[h] * cts[h] + kvs[h]
        n_ref[h] = n_news[h]
        m_ref[h] = m_news[h]


def _mlstm_core(qkv, og, gates, b_gate, head_norm, B, S):
    H = A_HEADS
    T = B * S
    dv = og.shape[1] // H
    dqk = dv // 2
    L = _tile(S, MLSTM_CHUNK)
    nc = S // L
    gw = gates.shape[1]
    gates_t = gates[:, :2 * H].T
    bias_c = jnp.pad(b_gate, (0, gw - 2 * H)).reshape(1, gw)
    kern = functools.partial(_mlstm_body, L=L, H=H, dqk=dqk, dv=dv)
    return pl.pallas_call(
        kern,
        grid=(B, nc),
        in_specs=[
            pl.BlockSpec((L, H * dqk), lambda b, c: (b * nc + c, 0)),
            pl.BlockSpec((L, H * dqk), lambda b, c: (b * nc + c, 1)),
            pl.BlockSpec((L, H * dv), lambda b, c: (b * nc + c, 1)),
            pl.BlockSpec((L, H * dv), lambda b, c: (b * nc + c, 0)),
            pl.BlockSpec((L, gw), lambda b, c: (b * nc + c, 0)),
            pl.BlockSpec((2 * H, L), lambda b, c: (0, b * nc + c)),
            pl.BlockSpec((1, gw), lambda b, c: (0, 0)),
            pl.BlockSpec((2 * H, 1), lambda b, c: (0, 0)),
            pl.BlockSpec((1, H * dv), lambda b, c: (0, 0)),
        ],
        out_specs=pl.BlockSpec((L, H * dv), lambda b, c: (b * nc + c, 0)),
        out_shape=jax.ShapeDtypeStruct((T, H * dv), BF16),
        scratch_shapes=[
            pltpu.VMEM((H, dqk, dv), F32),
            pltpu.VMEM((H, 1, dqk), F32),
            pltpu.VMEM((H, 1, 1), F32),
        ],
        compiler_params=_params("parallel", "arbitrary"),
        name="mlstm_core",
    )(qkv, qkv, qkv, og, gates, gates_t, bias_c, b_gate.reshape(2 * H, 1),
      head_norm.reshape(1, H * dv))


def _sb_body(q_ref, k_ref, v_ref, o_ref, *, QB, G, dh):
    qi = pl.program_id(2)
    row = lax.broadcasted_iota(jnp.int32, (QB, QB), 0)
    col = lax.broadcasted_iota(jnp.int32, (QB, QB), 1)
    strict = col < row
    later = (row > col).astype(BF16)
    later2 = jnp.concatenate([later, later], axis=0)

    def step(j, state, diag):
        start = pl.multiple_of(j * QB, QB)
        heads = [slice(g * dh, (g + 1) * dh) for g in range(G)]
        zs = [lax.dot_general(q_ref[:, hs], k_ref[pl.ds(start, QB), hs],
                              (((1,), (1,)), ((), ())), preferred_element_type=F32)
              for hs in heads]
        log_sigs, carries, pieces = [], [], []
        for g, z in enumerate(zs):
            neg = jnp.minimum(z, 0.0)
            pos = jnp.maximum(z, 0.0)
            lg = jnp.log(1.0 + jnp.exp2(neg - pos)) * LOG2E
            sp = pos + lg
            if diag:
                sp = jnp.where(strict, sp, 0.0)
            hi = sp.astype(BF16)
            lo = (sp - hi.astype(F32)).astype(BF16)
            log_sigs.append((neg - lg) - state[g][0])
            carries.append(state[g][0] + jnp.sum(sp, axis=1, keepdims=True))
            pieces.append(jnp.concatenate([hi, lo], axis=1))
        s_afters = [jnp.dot(p, later2, preferred_element_type=F32) for p in pieces]
        atts = []
        for g in range(G):
            att = jnp.exp2(log_sigs[g] - s_afters[g])
            if diag:
                att = jnp.where(strict, att, 0.0)
            atts.append(att.astype(BF16))
        new_state = []
        for g in range(G):
            acc = state[g][1] + jnp.dot(atts[g], v_ref[pl.ds(start, QB), heads[g]],
                                        preferred_element_type=F32)
            new_state.append((carries[g], acc))
        return tuple(new_state)

    state = tuple((jnp.zeros((QB, 1), F32), jnp.zeros((QB, dh), F32)) for _ in range(G))
    state = step(qi, state, True)
    state = lax.fori_loop(0, qi, lambda it, st: step(qi - 1 - it, st, False), state)
    o_ref[...] = jnp.concatenate([st[1] for st in state], axis=1).astype(o_ref.dtype)


def _stick_breaking_core(q, kv, B, S):
    H = B_HEADS
    G = SB_HEADS_PER_STEP
    T = B * S
    dh = q.shape[1] // H
    QB = _tile(S, SB_BLOCK)
    nq = S // QB
    kern = functools.partial(_sb_body, QB=QB, G=G, dh=dh)
    return pl.pallas_call(
        kern,
        grid=(B, H // G, nq),
        in_specs=[
            pl.BlockSpec((QB, G * dh), lambda b, h, i: (b * nq + i, h)),
            pl.BlockSpec((S, G * dh), lambda b, h, i: (b, h)),
            pl.BlockSpec((S, G * dh), lambda b, h, i: (b, H // G + h)),
        ],
        out_specs=pl.BlockSpec((QB, G * dh), lambda b, h, i: (b * nq + i, h)),
        out_shape=jax.ShapeDtypeStruct((T, H * dh), BF16),
        compiler_params=_params("parallel", "parallel", "arbitrary"),
        name="stick_breaking_core",
    )(q, kv, kv)


def kernel(x, ffn1_norm, ffn1_w_in, ffn1_w_out, mix_norm, ffn2_norm, ffn2_w_in, ffn2_w_out,
           a_w_in, a_b_gate, a_head_norm, a_w_out, kv_norm, kv_w, b_w_q, b_w_out, final_norm):
    B, S, D = x.shape
    depth = ffn1_norm.shape[0]
    n_a = a_w_in.shape[0]
    n_qk = A_HEADS * (D // A_HEADS // 2)
    n_qkv = 2 * n_qk + D
    xt = x.reshape(B * S, D)
    a_w_in_t = jnp.swapaxes(a_w_in, 1, 2)
    kv = None
    for l in range(depth):
        xt = _ffn(xt, ffn1_norm[l], ffn1_w_in, ffn1_w_out, l)
        if l < n_a:
            w_gate = jnp.pad(a_w_in_t[l, n_qkv + D:], ((0, LANES - 2 * A_HEADS), (0, 0)))
            qkv, og, gates = _mlstm_proj(xt, mix_norm[l], a_w_in_t, l, w_gate, n_qk, n_qkv, D,
                                         scale=(n_qk // A_HEADS) ** -0.5)
            hs = _mlstm_core(qkv, og, gates, a_b_gate[l], a_head_norm[l], B, S)
            xt = _matmul_residual(hs, a_w_out, l, xt)
        else:
            j = l - n_a
            q = _norm_matmul(xt, mix_norm[l], b_w_q, j, 0, D, BF16,
                             scale=(D // B_HEADS) ** -0.5 * LOG2E)
            o = _stick_breaking_core(q, kv, B, S)
            xt = _matmul_residual(o, b_w_out, j, xt)
        xt = _ffn(xt, ffn2_norm[l], ffn2_w_in, ffn2_w_out, l,
                  out_norm=final_norm if l == depth - 1 else None)
        if l == n_a - 1:
            kv = _norm_matmul(xt, kv_norm, kv_w[None], 0, 0, 2 * D, BF16)
    return xt.reshape(B, S, D)
```

```python
import functools

import jax
import jax.numpy as jnp
from jax import lax
from jax.experimental import pallas as pl
from jax.experimental.pallas import tpu as pltpu

EPS = 1e-6
GATE_SOFTCAP = 15.0
A_HEADS = 8
B_HEADS = 16
MLSTM_CHUNK = 256
SB_BLOCK = 256
SB_HEADS_PER_STEP = 4
LANES = 128
LOG2E = 1.4426950408889634
VMEM_LIMIT_BYTES = 56 * 1024 * 1024

F32 = jnp.float32
BF16 = jnp.bfloat16


def _params(*sem):
    return pltpu.CompilerParams(dimension_semantics=sem, vmem_limit_bytes=VMEM_LIMIT_BYTES)


def _tile(n, want):
    t = min(n, want)
    assert n % t == 0, (n, want)
    return t


def _rms_scale(x, g):
    ms = jnp.mean(x * x, axis=-1, keepdims=True)
    return x * lax.rsqrt(ms + EPS) * g


def _log_sigmoid(x):
    return jnp.minimum(x, 0.0) - jnp.log1p(jnp.exp(-jnp.abs(x)))


def _ffn_body(x_ref, g_ref, wg_ref, wu_ref, wo_ref, *rest, has_out_norm):
    if has_out_norm:
        go_ref, o_ref, xn_ref = rest
    else:
        o_ref, xn_ref = rest

    def half_swiglu(xn):
        gate = jnp.dot(xn, wg_ref[...].astype(BF16), preferred_element_type=F32)
        up = jnp.dot(xn, wu_ref[...].astype(BF16), preferred_element_type=F32)
        act = (0.5 * gate * jax.nn.sigmoid(gate) * up).astype(BF16)
        return jnp.dot(act, wo_ref[...].astype(BF16), preferred_element_type=F32)

    @pl.when(pl.program_id(1) == 0)
    def _():
        x = x_ref[...]
        xn = _rms_scale(x, g_ref[...]).astype(BF16)
        xn_ref[...] = xn
        o_ref[...] = x + half_swiglu(xn)

    @pl.when(pl.program_id(1) != 0)
    def _():
        o_ref[...] += half_swiglu(xn_ref[...])

    if has_out_norm:
        @pl.when(pl.program_id(1) == pl.num_programs(1) - 1)
        def _():
            o_ref[...] = _rms_scale(o_ref[...], go_ref[...])


def _ffn(x, g, w_in, w_out, layer, out_norm=None, *, tm=1024, tf=256):
    T, D = x.shape
    F = w_out.shape[1]
    tm, tf = _tile(T, tm), _tile(F, tf)
    nf = F // tf
    extra_specs, extra_args = [], []
    if out_norm is not None:
        extra_specs = [pl.BlockSpec((1, D), lambda i, j: (0, 0))]
        extra_args = [out_norm.reshape(1, D)]
    return pl.pallas_call(
        functools.partial(_ffn_body, has_out_norm=out_norm is not None),
        grid=(T // tm, nf),
        in_specs=[
            pl.BlockSpec((tm, D), lambda i, j: (i, 0)),
            pl.BlockSpec((1, D), lambda i, j: (0, 0)),
            pl.BlockSpec((None, D, tf), lambda i, j: (layer, 0, j)),
            pl.BlockSpec((None, D, tf), lambda i, j: (layer, 0, j + nf)),
            pl.BlockSpec((None, tf, D), lambda i, j: (layer, j, 0)),
        ] + extra_specs,
        out_specs=pl.BlockSpec((tm, D), lambda i, j: (i, 0)),
        out_shape=jax.ShapeDtypeStruct((T, D), F32),
        scratch_shapes=[pltpu.VMEM((tm, D), BF16)],
        compiler_params=_params("parallel", "arbitrary"),
        name="ffn",
    )(x, g.reshape(1, D), w_in, w_in, w_out, *extra_args)


def _norm_matmul_body(x_ref, g_ref, w_ref, o_ref, xn_ref, *, scale, scale_blocks, w_transposed):
    def project(xn):
        contract_w = 1 if w_transposed else 0
        y = lax.dot_general(xn, w_ref[...].astype(BF16), (((1,), (contract_w,)), ((), ())),
                            preferred_element_type=F32)
        if scale != 1.0:
            y = y * jnp.where(pl.program_id(1) < scale_blocks, scale, 1.0)
        o_ref[...] = y.astype(o_ref.dtype)

    @pl.when(pl.program_id(1) == 0)
    def _():
        xn = _rms_scale(x_ref[...], g_ref[...]).astype(BF16)
        xn_ref[...] = xn
        project(xn)

    @pl.when(pl.program_id(1) != 0)
    def _():
        project(xn_ref[...])


def _norm_matmul(x, g, w, layer, col0, n, out_dtype, *, scale=1.0, scale_cols=None,
                 w_transposed=False, tm=1024, tn=1024):
    T, D = x.shape
    tm, tn = _tile(T, tm), _tile(n, tn)
    scale_cols = n if scale_cols is None else scale_cols
    assert col0 % tn == 0 and scale_cols % tn == 0
    cb = col0 // tn
    return pl.pallas_call(
        functools.partial(_norm_matmul_body, scale=float(scale), scale_blocks=scale_cols // tn,
                          w_transposed=w_transposed),
        grid=(T // tm, n // tn),
        in_specs=[
            pl.BlockSpec((tm, D), lambda i, j: (i, 0)),
            pl.BlockSpec((1, D), lambda i, j: (0, 0)),
            (pl.BlockSpec((None, tn, D), lambda i, j: (layer, j + cb, 0)) if w_transposed else
             pl.BlockSpec((None, D, tn), lambda i, j: (layer, 0, j + cb))),
        ],
        out_specs=pl.BlockSpec((tm, tn), lambda i, j: (i, j)),
        out_shape=jax.ShapeDtypeStruct((T, n), out_dtype),
        scratch_shapes=[pltpu.VMEM((tm, D), BF16)],
        compiler_params=_params("parallel", "arbitrary"),
        name="norm_matmul",
    )(x, g.reshape(1, D), w)


def _mlstm_proj_body(x_ref, g_ref, w_ref, wgate_ref, qkv_ref, og_ref, gates_ref, xn_ref, *,
                     scale, scale_blocks, qkv_blocks):
    nt = (((1,), (1,)), ((), ()))
    j = pl.program_id(1)

    def project(xn):
        return lax.dot_general(xn, w_ref[...].astype(BF16), nt, preferred_element_type=F32)

    @pl.when(j == 0)
    def _():
        xn = _rms_scale(x_ref[...], g_ref[...]).astype(BF16)
        xn_ref[...] = xn
        gates_ref[...] = lax.dot_general(xn, wgate_ref[...].astype(BF16), nt,
                                         preferred_element_type=F32)
        qkv_ref[...] = (project(xn) * scale).astype(qkv_ref.dtype)

    @pl.when((j != 0) & (j < qkv_blocks))
    def _():
        y = project(xn_ref[...]) * jnp.where(j < scale_blocks, scale, 1.0)
        qkv_ref[...] = y.astype(qkv_ref.dtype)

    @pl.when(j >= qkv_blocks)
    def _():
        og_ref[...] = project(xn_ref[...])


def _mlstm_proj(x, g, w_t, layer, w_gate, n_q, n_qkv, n_og, *, scale, tm=1024, tn=512):
    T, D = x.shape
    tm = _tile(T, tm)
    assert n_q % tn == 0 and n_qkv % tn == 0 and n_og % tn == 0 and n_q >= tn
    qb, ob = n_qkv // tn, n_og // tn
    return pl.pallas_call(
        functools.partial(_mlstm_proj_body, scale=float(scale), scale_blocks=n_q // tn,
                          qkv_blocks=qb),
        grid=(T // tm, qb + ob),
        in_specs=[
            pl.BlockSpec((tm, D), lambda i, j: (i, 0)),
            pl.BlockSpec((1, D), lambda i, j: (0, 0)),
            pl.BlockSpec((None, tn, D), lambda i, j: (layer, j, 0)),
            pl.BlockSpec((LANES, D), lambda i, j: (0, 0)),
        ],
        out_specs=[
            pl.BlockSpec((tm, tn), lambda i, j: (i, jnp.minimum(j, qb - 1))),
            pl.BlockSpec((tm, tn), lambda i, j: (i, jnp.maximum(j - qb, 0))),
            pl.BlockSpec((tm, LANES), lambda i, j: (i, 0)),
        ],
        out_shape=[jax.ShapeDtypeStruct((T, n_qkv), BF16),
                   jax.ShapeDtypeStruct((T, n_og), F32),
                   jax.ShapeDtypeStruct((T, LANES), F32)],
        scratch_shapes=[pltpu.VMEM((tm, D), BF16)],
        compiler_params=_params("parallel", "arbitrary"),
        name="mlstm_proj",
    )(x, g.reshape(1, D), w_t, w_gate)


def _matmul_residual_body(h_ref, w_ref, x_ref, o_ref):
    o_ref[...] = x_ref[...] + jnp.dot(h_ref[...], w_ref[...].astype(BF16),
                                      preferred_element_type=F32)


def _matmul_residual(h, w, layer, x, *, tm=2048, tn=512):
    T, K = h.shape
    D = x.shape[1]
    tm, tn = _tile(T, tm), _tile(D, tn)
    return pl.pallas_call(
        _matmul_residual_body,
        grid=(T // tm, D // tn),
        in_specs=[
            pl.BlockSpec((tm, K), lambda i, j: (i, 0)),
            pl.BlockSpec((None, K, tn), lambda i, j: (layer, 0, j)),
            pl.BlockSpec((tm, tn), lambda i, j: (i, j)),
        ],
        out_specs=pl.BlockSpec((tm, tn), lambda i, j: (i, j)),
        out_shape=jax.ShapeDtypeStruct((T, D), F32),
        compiler_params=_params("parallel", "arbitrary"),
        name="matmul_residual",
    )(h, w, x)


def _split3(x):
    h1 = x.astype(BF16)
    r1 = x - h1.astype(F32)
    h2 = r1.astype(BF16)
    h3 = (r1 - h2.astype(F32)).astype(BF16)
    return h1, h2, h3


def _mlstm_body(q_ref, k_ref, v_ref, og_ref, gc_ref, gr_ref, bc_ref, br_ref, hn_ref,
                o_ref, c_ref, n_ref, m_ref, *, L, H, dqk, dv):
    @pl.when(pl.program_id(1) == 0)
    def _():
        c_ref[...] = jnp.zeros(c_ref.shape, F32)
        n_ref[...] = jnp.zeros(n_ref.shape, F32)
        m_ref[...] = jnp.zeros(m_ref.shape, F32)

    row = lax.broadcasted_iota(jnp.int32, (L, L), 0)
    col = lax.broadcasted_iota(jnp.int32, (L, L), 1)
    causal = col <= row
    tri = causal.astype(BF16)

    gcol = gc_ref[...] + bc_ref[...]
    ig_c = GATE_SOFTCAP * jnp.tanh(gcol * (1.0 / GATE_SOFTCAP))
    lf_c = _log_sigmoid(gcol)
    b_c = sum(jnp.dot(tri, p, preferred_element_type=F32) for p in _split3(lf_c))
    grow = gr_ref[...] + br_ref[...]
    ig_r = GATE_SOFTCAP * jnp.tanh(grow * (1.0 / GATE_SOFTCAP))
    lf_r = _log_sigmoid(grow)
    b_r = sum(lax.dot_general(p, tri, (((1,), (1,)), ((), ())), preferred_element_type=F32)
              for p in _split3(lf_r))

    qs = [q_ref[:, h * dqk:(h + 1) * dqk] for h in range(H)]
    ks = [k_ref[:, h * dqk:(h + 1) * dqk] for h in range(H)]
    vs = [v_ref[:, h * dv:(h + 1) * dv] for h in range(H)]
    cts = [c_ref[h] for h in range(H)]
    ss = [lax.dot_general(qs[h], ks[h], (((1,), (1,)), ((), ())), preferred_element_type=F32)
          for h in range(H)]
    qcs = [jnp.dot(qs[h], cts[h].astype(BF16), preferred_element_type=F32) for h in range(H)]

    sws, decs, invs, wvs, cdecs, n_news, m_news = [], [], [], [], [], [], []
    for h in range(H):
        bcol = b_c[:, H + h:H + h + 1]
        icol = ig_c[:, h:h + 1]
        brow = b_r[H + h:H + h + 1, :]
        irow = ig_r[h:h + 1, :]
        m_prev = m_ref[h]
        n_prev = n_ref[h]
        dm = jnp.where(causal, bcol - brow + irow, -jnp.inf)
        m_inter = bcol + m_prev
        m_t = jnp.maximum(jnp.max(dm, axis=1, keepdims=True), m_inter)
        sw = ss[h] * jnp.exp(dm - m_t)
        dec = jnp.exp(m_inter - m_t)
        nq = jnp.sum(qs[h].astype(F32) * n_prev, axis=1, keepdims=True)
        den = jnp.sum(sw, axis=1, keepdims=True) + dec * nq
        invs.append(1.0 / jnp.maximum(jnp.abs(den), jnp.exp(-m_t)))
        sws.append(sw.astype(BF16))
        decs.append(dec)
        b_last = bcol[L - 1:L, :]
        gdec = b_last - bcol + icol
        m_new = jnp.maximum(b_last + m_prev, jnp.max(gdec, axis=0, keepdims=True))
        w = jnp.exp(gdec - m_new)
        cdec = jnp.exp(b_last + m_prev - m_new)
        wvs.append((w * vs[h].astype(F32)).astype(BF16))
        n_news.append(cdec * n_prev + jnp.sum(w * ks[h].astype(F32), axis=0, keepdims=True))
        cdecs.append(cdec)
        m_news.append(m_new)

    nums = [jnp.dot(sws[h], vs[h], preferred_element_type=F32) for h in range(H)]
    kvs = [lax.dot_general(ks[h], wvs[h], (((0,), (0,)), ((), ())), preferred_element_type=F32)
           for h in range(H)]

    for h in range(H):
        hd = slice(h * dv, (h + 1) * dv)
        out = (nums[h] + decs[h] * qcs[h]) * invs[h]
        hs = out * lax.rsqrt(jnp.mean(out * out, axis=1, keepdims=True) + EPS)
        hs = hs * hn_ref[:, hd] * jax.nn.sigmoid(og_ref[:, hd])
        o_ref[:, hd] = hs.astype(o_ref.dtype)
        c_ref[h] = cdecs[h] * cts[h] + kvs[h]
        n_ref[h] = n_news[h]
        m_ref[h] = m_news[h]


def _mlstm_core(qkv, og, gates, b_gate, head_norm, B, S):
    H = A_HEADS
    T = B * S
    dv = og.shape[1] // H
    dqk = dv // 2
    L = _tile(S, MLSTM_CHUNK)
    nc = S // L
    gw = gates.shape[1]
    gates_t = gates[:, :2 * H].T
    bias_c = jnp.pad(b_gate, (0, gw - 2 * H)).reshape(1, gw)
    kern = functools.partial(_mlstm_body, L=L, H=H, dqk=dqk, dv=dv)
    return pl.pallas_call(
        kern,
        grid=(B, nc),
        in_specs=[
            pl.BlockSpec((L, H * dqk), lambda b, c: (b * nc + c, 0)),
            pl.BlockSpec((L, H * dqk), lambda b, c: (b * nc + c, 1)),
            pl.BlockSpec((L, H * dv), lambda b, c: (b * nc + c, 1)),
            pl.BlockSpec((L, H * dv), lambda b, c: (b * nc + c, 0)),
            pl.BlockSpec((L, gw), lambda b, c: (b * nc + c, 0)),
            pl.BlockSpec((2 * H, L), lambda b, c: (0, b * nc + c)),
            pl.BlockSpec((1, gw), lambda b, c: (0, 0)),
            pl.BlockSpec((2 * H, 1), lambda b, c: (0, 0)),
            pl.BlockSpec((1, H * dv), lambda b, c: (0, 0)),
        ],
        out_specs=pl.BlockSpec((L, H * dv), lambda b, c: (b * nc + c, 0)),
        out_shape=jax.ShapeDtypeStruct((T, H * dv), BF16),
        scratch_shapes=[
            pltpu.VMEM((H, dqk, dv), F32),
            pltpu.VMEM((H, 1, dqk), F32),
            pltpu.VMEM((H, 1, 1), F32),
        ],
        compiler_params=_params("parallel", "arbitrary"),
        name="mlstm_core",
    )(qkv, qkv, qkv, og, gates, gates_t, bias_c, b_gate.reshape(2 * H, 1),
      head_norm.reshape(1, H * dv))


def _sb_body(q_ref, k_ref, v_ref, o_ref, *, QB, G, dh):
    qi = pl.program_id(2)
    row = lax.broadcasted_iota(jnp.int32, (QB, QB), 0)
    col = lax.broadcasted_iota(jnp.int32, (QB, QB), 1)
    strict = col < row
    later = (row > col).astype(BF16)
    later2 = jnp.concatenate([later, later], axis=0)

    def step(j, state, diag):
        start = pl.multiple_of(j * QB, QB)
        heads = [slice(g * dh, (g + 1) * dh) for g in range(G)]
        zs = [lax.dot_general(q_ref[:, hs], k_ref[pl.ds(start, QB), hs],
                              (((1,), (1,)), ((), ())), preferred_element_type=F32)
              for hs in heads]
        log_sigs, carries, pieces = [], [], []
        for g, z in enumerate(zs):
            neg = jnp.minimum(z, 0.0)
            pos = jnp.maximum(z, 0.0)
            lg = jnp.log(1.0 + jnp.exp2(neg - pos)) * LOG2E
            sp = pos + lg
            if diag:
                sp = jnp.where(strict, sp, 0.0)
            hi = sp.astype(BF16)
            lo = (sp - hi.astype(F32)).astype(BF16)
            log_sigs.append((neg - lg) - state[g][0])
            carries.append(state[g][0] + jnp.sum(sp, axis=1, keepdims=True))
            pieces.append(jnp.concatenate([hi, lo], axis=1))
        s_afters = [jnp.dot(p, later2, preferred_element_type=F32) for p in pieces]
        atts = []
        for g in range(G):
            att = jnp.exp2(log_sigs[g] - s_afters[g])
            if diag:
                att = jnp.where(strict, att, 0.0)
            atts.append(att.astype(BF16))
        new_state = []
        for g in range(G):
            acc = state[g][1] + jnp.dot(atts[g], v_ref[pl.ds(start, QB), heads[g]],
                                        preferred_element_type=F32)
            new_state.append((carries[g], acc))
        return tuple(new_state)

    state = tuple((jnp.zeros((QB, 1), F32), jnp.zeros((QB, dh), F32)) for _ in range(G))
    state = step(qi, state, True)
    state = lax.fori_loop(0, qi, lambda it, st: step(qi - 1 - it, st, False), state)
    o_ref[...] = jnp.concatenate([st[1] for st in state], axis=1).astype(o_ref.dtype)


def _stick_breaking_core(q, kv, B, S):
    H = B_HEADS
    G = SB_HEADS_PER_STEP
    T = B * S
    dh = q.shape[1] // H
    QB = _tile(S, SB_BLOCK)
    nq = S // QB
    kern = functools.partial(_sb_body, QB=QB, G=G, dh=dh)
    return pl.pallas_call(
        kern,
        grid=(B, H // G, nq),
        in_specs=[
            pl.BlockSpec((QB, G * dh), lambda b, h, i: (b * nq + i, h)),
            pl.BlockSpec((S, G * dh), lambda b, h, i: (b, h)),
            pl.BlockSpec((S, G * dh), lambda b, h, i: (b, H // G + h)),
        ],
        out_specs=pl.BlockSpec((QB, G * dh), lambda b, h, i: (b * nq + i, h)),
        out_shape=jax.ShapeDtypeStruct((T, H * dh), BF16),
        compiler_params=_params("parallel", "parallel", "arbitrary"),
        name="stick_breaking_core",
    )(q, kv, kv)


def kernel(x, ffn1_norm, ffn1_w_in, ffn1_w_out, mix_norm, ffn2_norm, ffn2_w_in, ffn2_w_out,
           a_w_in, a_b_gate, a_head_norm, a_w_out, kv_norm, kv_w, b_w_q, b_w_out, final_norm):
    B, S, D = x.shape
    depth = ffn1_norm.shape[0]
    n_a = a_w_in.shape[0]
    n_qk = A_HEADS * (D // A_HEADS // 2)
    n_qkv = 2 * n_qk + D
    xt = x.reshape(B * S, D)
    a_w_in_t = jnp.swapaxes(a_w_in, 1, 2)
    kv = None
    for l in range(depth):
        xt = _ffn(xt, ffn1_norm[l], ffn1_w_in, ffn1_w_out, l)
        if l < n_a:
            w_gate = jnp.pad(a_w_in_t[l, n_qkv + D:], ((0, LANES - 2 * A_HEADS), (0, 0)))
            qkv, og, gates = _mlstm_proj(xt, mix_norm[l], a_w_in_t, l, w_gate, n_qk, n_qkv, D,
                                         scale=(n_qk // A_HEADS) ** -0.5)
            hs = _mlstm_core(qkv, og, gates, a_b_gate[l], a_head_norm[l], B, S)
            xt = _matmul_residual(hs, a_w_out, l, xt)
        else:
            j = l - n_a
            q = _norm_matmul(xt, mix_norm[l], b_w_q, j, 0, D, BF16,
                             scale=(D // B_HEADS) ** -0.5 * LOG2E)
            o = _stick_breaking_core(q, kv, B, S)
            xt = _matmul_residual(o, b_w_out, j, xt)
        xt = _ffn(xt, ffn2_norm[l], ffn2_w_in, ffn2_w_out, l,
                  out_norm=final_norm if l == depth - 1 else None)
        if l == n_a - 1:
            kv = _norm_matmul(xt, kv_norm, kv_w[None], 0, 0, 2 * D, BF16)
    return xt.reshape(B, S, D)
```

```python
import functools

import jax
import jax.numpy as jnp
from jax import lax
from jax.experimental import pallas as pl
from jax.experimental.pallas import tpu as pltpu

EPS = 1e-6
GATE_SOFTCAP = 15.0
A_HEADS = 8
B_HEADS = 16
MLSTM_CHUNK = 256
SB_BLOCK = 256
SB_HEADS_PER_STEP = 4
LANES = 128
LOG2E = 1.4426950408889634
VMEM_LIMIT_BYTES = 56 * 1024 * 1024

F32 = jnp.float32
BF16 = jnp.bfloat16


def _params(*sem):
    return pltpu.CompilerParams(dimension_semantics=sem, vmem_limit_bytes=VMEM_LIMIT_BYTES)


def _tile(n, want):
    t = min(n, want)
    assert n % t == 0, (n, want)
    return t


def _rms_scale(x, g):
    ms = jnp.mean(x * x, axis=-1, keepdims=True)
    return x * lax.rsqrt(ms + EPS) * g


def _log_sigmoid(x):
    return jnp.minimum(x, 0.0) - jnp.log1p(jnp.exp(-jnp.abs(x)))


def _ffn_body(x_ref, g_ref, wg_ref, wu_ref, wo_ref, *rest, has_out_norm):
    if has_out_norm:
        go_ref, o_ref, xn_ref = rest
    else:
        o_ref, xn_ref = rest

    def half_swiglu(xn):
        gate = jnp.dot(xn, wg_ref[...].astype(BF16), preferred_element_type=F32)
        up = jnp.dot(xn, wu_ref[...].astype(BF16), preferred_element_type=F32)
        act = (0.5 * gate * jax.nn.sigmoid(gate) * up).astype(BF16)
        return jnp.dot(act, wo_ref[...].astype(BF16), preferred_element_type=F32)

    @pl.when(pl.program_id(1) == 0)
    def _():
        x = x_ref[...]
        xn = _rms_scale(x, g_ref[...]).astype(BF16)
        xn_ref[...] = xn
        o_ref[...] = x + half_swiglu(xn)

    @pl.when(pl.program_id(1) != 0)
    def _():
        o_ref[...] += half_swiglu(xn_ref[...])

    if has_out_norm:
        @pl.when(pl.program_id(1) == pl.num_programs(1) - 1)
        def _():
            o_ref[...] = _rms_scale(o_ref[...], go_ref[...])


def _ffn(x, g, w_in, w_out, layer, out_norm=None, *, tm=1024, tf=256):
    T, D = x.shape
    F = w_out.shape[1]
    tm, tf = _tile(T, tm), _tile(F, tf)
    nf = F // tf
    extra_specs, extra_args = [], []
    if out_norm is not None:
        extra_specs = [pl.BlockSpec((1, D), lambda i, j: (0, 0))]
        extra_args = [out_norm.reshape(1, D)]
    return pl.pallas_call(
        functools.partial(_ffn_body, has_out_norm=out_norm is not None),
        grid=(T // tm, nf),
        in_specs=[
            pl.BlockSpec((tm, D), lambda i, j: (i, 0)),
            pl.BlockSpec((1, D), lambda i, j: (0, 0)),
            pl.BlockSpec((None, D, tf), lambda i, j: (layer, 0, j)),
            pl.BlockSpec((None, D, tf), lambda i, j: (layer, 0, j + nf)),
            pl.BlockSpec((None, tf, D), lambda i, j: (layer, j, 0)),
        ] + extra_specs,
        out_specs=pl.BlockSpec((tm, D), lambda i, j: (i, 0)),
        out_shape=jax.ShapeDtypeStruct((T, D), F32),
        scratch_shapes=[pltpu.VMEM((tm, D), BF16)],
        compiler_params=_params("parallel", "arbitrary"),
        name="ffn",
    )(x, g.reshape(1, D), w_in, w_in, w_out, *extra_args)


def _norm_matmul_body(x_ref, g_ref, w_ref, o_ref, xn_ref, *, scale, scale_blocks, w_transposed):
    def project(xn):
        contract_w = 1 if w_transposed else 0
        y = lax.dot_general(xn, w_ref[...].astype(BF16), (((1,), (contract_w,)), ((), ())),
                            preferred_element_type=F32)
        if scale != 1.0:
            y = y * jnp.where(pl.program_id(1) < scale_blocks, scale, 1.0)
        o_ref[...] = y.astype(o_ref.dtype)

    @pl.when(pl.program_id(1) == 0)
    def _():
        xn = _rms_scale(x_ref[...], g_ref[...]).astype(BF16)
        xn_ref[...] = xn
        project(xn)

    @pl.when(pl.program_id(1) != 0)
    def _():
        project(xn_ref[...])


def _norm_matmul(x, g, w, layer, col0, n, out_dtype, *, scale=1.0, scale_cols=None,
                 w_transposed=False, tm=1024, tn=1024):
    T, D = x.shape
    tm, tn = _tile(T, tm), _tile(n, tn)
    scale_cols = n if scale_cols is None else scale_cols
    assert col0 % tn == 0 and scale_cols % tn == 0
    cb = col0 // tn
    return pl.pallas_call(
        functools.partial(_norm_matmul_body, scale=float(scale), scale_blocks=scale_cols // tn,
                          w_transposed=w_transposed),
        grid=(T // tm, n // tn),
        in_specs=[
            pl.BlockSpec((tm, D), lambda i, j: (i, 0)),
            pl.BlockSpec((1, D), lambda i, j: (0, 0)),
            (pl.BlockSpec((None, tn, D), lambda i, j: (layer, j + cb, 0)) if w_transposed else
             pl.BlockSpec((None, D, tn), lambda i, j: (layer, 0, j + cb))),
        ],
        out_specs=pl.BlockSpec((tm, tn), lambda i, j: (i, j)),
        out_shape=jax.ShapeDtypeStruct((T, n), out_dtype),
        scratch_shapes=[pltpu.VMEM((tm, D), BF16)],
        compiler_params=_params("parallel", "arbitrary"),
        name="norm_matmul",
    )(x, g.reshape(1, D), w)


def _mlstm_proj_body(x_ref, g_ref, w_ref, wgate_ref, qkv_ref, og_ref, gates_ref, xn_ref, *,
                     scale, scale_blocks, qkv_blocks):
    nt = (((1,), (1,)), ((), ()))
    j = pl.program_id(1)

    def project(xn):
        return lax.dot_general(xn, w_ref[...].astype(BF16), nt, preferred_element_type=F32)

    @pl.when(j == 0)
    def _():
        xn = _rms_scale(x_ref[...], g_ref[...]).astype(BF16)
        xn_ref[...] = xn
        gates_ref[...] = lax.dot_general(xn, wgate_ref[...].astype(BF16), nt,
                                         preferred_element_type=F32)
        qkv_ref[...] = (project(xn) * scale).astype(qkv_ref.dtype)

    @pl.when((j != 0) & (j < qkv_blocks))
    def _():
        y = project(xn_ref[...]) * jnp.where(j < scale_blocks, scale, 1.0)
        qkv_ref[...] = y.astype(qkv_ref.dtype)

    @pl.when(j >= qkv_blocks)
    def _():
        og_ref[...] = project(xn_ref[...])


def _mlstm_proj(x, g, w_t, layer, w_gate, n_q, n_qkv, n_og, *, scale, tm=1024, tn=512):
    T, D = x.shape
    tm = _tile(T, tm)
    assert n_q % tn == 0 and n_qkv % tn == 0 and n_og % tn == 0 and n_q >= tn
    qb, ob = n_qkv // tn, n_og // tn
    return pl.pallas_call(
        functools.partial(_mlstm_proj_body, scale=float(scale), scale_blocks=n_q // tn,
                          qkv_blocks=qb),
        grid=(T // tm, qb + ob),
        in_specs=[
            pl.BlockSpec((tm, D), lambda i, j: (i, 0)),
            pl.BlockSpec((1, D), lambda i, j: (0, 0)),
            pl.BlockSpec((None, tn, D), lambda i, j: (layer, j, 0)),
            pl.BlockSpec((LANES, D), lambda i, j: (0, 0)),
        ],
        out_specs=[
            pl.BlockSpec((tm, tn), lambda i, j: (i, jnp.minimum(j, qb - 1))),
            pl.BlockSpec((tm, tn), lambda i, j: (i, jnp.maximum(j - qb, 0))),
            pl.BlockSpec((tm, LANES), lambda i, j: (i, 0)),
        ],
        out_shape=[jax.ShapeDtypeStruct((T, n_qkv), BF16),
                   jax.ShapeDtypeStruct((T, n_og), F32),
                   jax.ShapeDtypeStruct((T, LANES), F32)],
        scratch_shapes=[pltpu.VMEM((tm, D), BF16)],
        compiler_params=_params("parallel", "arbitrary"),
        name="mlstm_proj",
    )(x, g.reshape(1, D), w_t, w_gate)


def _matmul_residual_body(h_ref, w_ref, x_ref, o_ref):
    o_ref[...] = x_ref[...] + jnp.dot(h_ref[...], w_ref[...].astype(BF16),
                                      preferred_element_type=F32)


def _matmul_residual(h, w, layer, x, *, tm=2048, tn=512):
    T, K = h.shape
    D = x.shape[1]
    tm, tn = _tile(T, tm), _tile(D, tn)
    return pl.pallas_call(
        _matmul_residual_body,
        grid=(T // tm, D // tn),
        in_specs=[
            pl.BlockSpec((tm, K), lambda i, j: (i, 0)),
            pl.BlockSpec((None, K, tn), lambda i, j: (layer, 0, j)),
            pl.BlockSpec((tm, tn), lambda i, j: (i, j)),
        ],
        out_specs=pl.BlockSpec((tm, tn), lambda i, j: (i, j)),
        out_shape=jax.ShapeDtypeStruct((T, D), F32),
        compiler_params=_params("parallel", "arbitrary"),
        name="matmul_residual",
    )(h, w, x)


def _split3(x):
    h1 = x.astype(BF16)
    r1 = x - h1.astype(F32)
    h2 = r1.astype(BF16)
    h3 = (r1 - h2.astype(F32)).astype(BF16)
    return h1, h2, h3


def _mlstm_body(q_ref, k_ref, v_ref, og_ref, gc_ref, gr_ref, bc_ref, br_ref, hn_ref, x_ref, wb_ref,
                o_ref, c_ref, n_ref, m_ref, hs_ref, *, L, H, dqk, dv, nc, out_chunks):
    t = pl.program_id(0)

    @pl.when(t == 0)
    def _():
        hs_ref[...] = jnp.zeros(hs_ref.shape, hs_ref.dtype)

    @pl.when(lax.rem(t, nc) == 0)
    def _():
        c_ref[...] = jnp.zeros(c_ref.shape, F32)
        n_ref[...] = jnp.zeros(n_ref.shape, F32)
        m_ref[...] = jnp.zeros(m_ref.shape, F32)

    hs_prev = hs_ref[...]
    dn = wb_ref.shape[1] // out_chunks

    def out_proj(i):
        cols = slice(i * dn, (i + 1) * dn)
        o_ref[:, cols] = x_ref[:, cols] + jnp.dot(hs_prev, wb_ref[:, cols],
                                                  preferred_element_type=F32)

    row = lax.broadcasted_iota(jnp.int32, (L, L), 0)
    col = lax.broadcasted_iota(jnp.int32, (L, L), 1)
    causal = col <= row
    tri = causal.astype(BF16)

    gcol = gc_ref[...] + bc_ref[...]
    ig_c = GATE_SOFTCAP * jnp.tanh(gcol * (1.0 / GATE_SOFTCAP))
    lf_c = _log_sigmoid(gcol)
    b_c = sum(jnp.dot(tri, p, preferred_element_type=F32) for p in _split3(lf_c))
    grow = gr_ref[...] + br_ref[...]
    ig_r = GATE_SOFTCAP * jnp.tanh(grow * (1.0 / GATE_SOFTCAP))
    lf_r = _log_sigmoid(grow)
    b_r = sum(lax.dot_general(p, tri, (((1,), (1,)), ((), ())), preferred_element_type=F32)
              for p in _split3(lf_r))

    qs = [q_ref[:, h * dqk:(h + 1) * dqk] for h in range(H)]
    ks = [k_ref[:, h * dqk:(h + 1) * dqk] for h in range(H)]
    vs = [v_ref[:, h * dv:(h + 1) * dv] for h in range(H)]
    cts = [c_ref[h] for h in range(H)]
    ss = [lax.dot_general(qs[h], ks[h], (((1,), (1,)), ((), ())), preferred_element_type=F32)
          for h in range(H)]
    qcs = [jnp.dot(qs[h], cts[h].astype(BF16), preferred_element_type=F32) for h in range(H)]
    out_proj(0)

    sws, decs, invs, wvs, cdecs, n_news, m_news = [], [], [], [], [], [], []
    for h in range(H):
        bcol = b_c[:, H + h:H + h + 1]
        icol = ig_c[:, h:h + 1]
        brow = b_r[H + h:H + h + 1, :]
        irow = ig_r[h:h + 1, :]
        m_prev = m_ref[h]
        n_prev = n_ref[h]
        dm = jnp.where(causal, bcol - brow + irow, -jnp.inf)
        m_inter = bcol + m_prev
        m_t = jnp.maximum(jnp.max(dm, axis=1, keepdims=True), m_inter)
        sw = ss[h] * jnp.exp(dm - m_t)
        dec = jnp.exp(m_inter - m_t)
        nq = jnp.sum(qs[h].astype(F32) * n_prev, axis=1, keepdims=True)
        den = jnp.sum(sw, axis=1, keepdims=True) + dec * nq
        invs.append(1.0 / jnp.maximum(jnp.abs(den), jnp.exp(-m_t)))
        sws.append(sw.astype(BF16))
        decs.append(dec)
        b_last = bcol[L - 1:L, :]
        gdec = b_last - bcol + icol
        m_new = jnp.maximum(b_last + m_prev, jnp.max(gdec, axis=0, keepdims=True))
        w = jnp.exp(gdec - m_new)
        cdec = jnp.exp(b_last + m_prev - m_new)
        wvs.append((w * vs[h].astype(F32)).astype(BF16))
        n_news.append(cdec * n_prev + jnp.sum(w * ks[h].astype(F32), axis=0, keepdims=True))
        cdecs.append(cdec)
        m_news.append(m_new)

    nums = [jnp.dot(sws[h], vs[h], preferred_element_type=F32) for h in range(H)]
    kvs = [lax.dot_general(ks[h], wvs[h], (((0,), (0,)), ((), ())), preferred_element_type=F32)
           for h in range(H)]
    for i in range(1, out_chunks):
        out_proj(i)

    for h in range(H):
        hd = slice(h * dv, (h + 1) * dv)
        out = (nums[h] + decs[h] * qcs[h]) * invs[h]
        hs = out * lax.rsqrt(jnp.mean(out * out, axis=1, keepdims=True) + EPS)
        hs = hs * hn_ref[:, hd] * jax.nn.sigmoid(og_ref[:, hd])
        hs_ref[:, hd] = hs.astype(hs_ref.dtype)
        c_ref[h] = cdecs[h] * cts[h] + kvs[h]
        n_ref[h] = n_news[h]
        m_ref[h] = m_news[h]


def _mlstm_mix(qkv, og, gates, b_gate, head_norm, x, w_out, B, S):
    H = A_HEADS
    T, D = x.shape
    dv = og.shape[1] // H
    dqk = dv // 2
    L = _tile(S, MLSTM_CHUNK)
    nc = S // L
    n = B * nc
    gw = gates.shape[1]
    gates_t = gates[:, :2 * H].T
    bias_c = jnp.pad(b_gate, (0, gw - 2 * H)).reshape(1, gw)
    kern = functools.partial(_mlstm_body, L=L, H=H, dqk=dqk, dv=dv, nc=nc, out_chunks=4)
    cur = lambda t: jnp.minimum(t, n - 1)
    prev = lambda t: jnp.maximum(t - 1, 0)
    return pl.pallas_call(
        kern,
        grid=(n + 1,),
        in_specs=[
            pl.BlockSpec((L, H * dqk), lambda t: (cur(t), 0)),
            pl.BlockSpec((L, H * dqk), lambda t: (cur(t), 1)),
            pl.BlockSpec((L, H * dv), lambda t: (cur(t), 1)),
            pl.BlockSpec((L, H * dv), lambda t: (cur(t), 0)),
            pl.BlockSpec((L, gw), lambda t: (cur(t), 0)),
            pl.BlockSpec((2 * H, L), lambda t: (0, cur(t))),
            pl.BlockSpec((1, gw), lambda t: (0, 0)),
            pl.BlockSpec((2 * H, 1), lambda t: (0, 0)),
            pl.BlockSpec((1, H * dv), lambda t: (0, 0)),
            pl.BlockSpec((L, D), lambda t: (prev(t), 0)),
            pl.BlockSpec((H * dv, D), lambda t: (0, 0), pipeline_mode=pl.Buffered(1)),
        ],
        out_specs=pl.BlockSpec((L, D), lambda t: (prev(t), 0)),
        out_shape=jax.ShapeDtypeStruct((T, D), F32),
        scratch_shapes=[
            pltpu.VMEM((H, dqk, dv), F32),
            pltpu.VMEM((H, 1, dqk), F32),
            pltpu.VMEM((H, 1, 1), F32),
            pltpu.VMEM((L, H * dv), BF16),
        ],
        compiler_params=_params("arbitrary"),
        name="mlstm_mix",
    )(qkv, qkv, qkv, og, gates, gates_t, bias_c, b_gate.reshape(2 * H, 1),
      head_norm.reshape(1, H * dv), x, w_out)


def _sb_body(q_ref, k_ref, v_ref, o_ref, *, QB, G, dh):
    qi = pl.program_id(2)
    row = lax.broadcasted_iota(jnp.int32, (QB, QB), 0)
    col = lax.broadcasted_iota(jnp.int32, (QB, QB), 1)
    strict = col < row
    later = (row > col).astype(BF16)
    later2 = jnp.concatenate([later, later], axis=0)

    def step(j, state, diag):
        start = pl.multiple_of(j * QB, QB)
        heads = [slice(g * dh, (g + 1) * dh) for g in range(G)]
        zs = [lax.dot_general(q_ref[:, hs], k_ref[pl.ds(start, QB), hs],
                              (((1,), (1,)), ((), ())), preferred_element_type=F32)
              for hs in heads]
        log_sigs, carries, pieces = [], [], []
        for g, z in enumerate(zs):
            neg = jnp.minimum(z, 0.0)
            pos = jnp.maximum(z, 0.0)
            lg = jnp.log(1.0 + jnp.exp2(neg - pos)) * LOG2E
            sp = pos + lg
            if diag:
                sp = jnp.where(strict, sp, 0.0)
            hi = sp.astype(BF16)
            lo = (sp - hi.astype(F32)).astype(BF16)
            log_sigs.append((neg - lg) - state[g][0])
            carries.append(state[g][0] + jnp.sum(sp, axis=1, keepdims=True))
            pieces.append(jnp.concatenate([hi, lo], axis=1))
        s_afters = [jnp.dot(p, later2, preferred_element_type=F32) for p in pieces]
        atts = []
        for g in range(G):
            att = jnp.exp2(log_sigs[g] - s_afters[g])
            if diag:
                att = jnp.where(strict, att, 0.0)
            atts.append(att.astype(BF16))
        new_state = []
        for g in range(G):
            acc = state[g][1] + jnp.dot(atts[g], v_ref[pl.ds(start, QB), heads[g]],
                                        preferred_element_type=F32)
            new_state.append((carries[g], acc))
        return tuple(new_state)

    state = tuple((jnp.zeros((QB, 1), F32), jnp.zeros((QB, dh), F32)) for _ in range(G))
    state = step(qi, state, True)
    state = lax.fori_loop(0, qi, lambda it, st: step(qi - 1 - it, st, False), state)
    o_ref[...] = jnp.concatenate([st[1] for st in state], axis=1).astype(o_ref.dtype)


def _stick_breaking_core(q, kv, B, S):
    H = B_HEADS
    G = SB_HEADS_PER_STEP
    T = B * S
    dh = q.shape[1] // H
    QB = _tile(S, SB_BLOCK)
    nq = S // QB
    kern = functools.partial(_sb_body, QB=QB, G=G, dh=dh)
    return pl.pallas_call(
        kern,
        grid=(B, H // G, nq),
        in_specs=[
            pl.BlockSpec((QB, G * dh), lambda b, h, i: (b * nq + i, h)),
            pl.BlockSpec((S, G * dh), lambda b, h, i: (b, h)),
            pl.BlockSpec((S, G * dh), lambda b, h, i: (b, H // G + h)),
        ],
        out_specs=pl.BlockSpec((QB, G * dh), lambda b, h, i: (b * nq + i, h)),
        out_shape=jax.ShapeDtypeStruct((T, H * dh), BF16),
        compiler_params=_params("parallel", "parallel", "arbitrary"),
        name="stick_breaking_core",
    )(q, kv, kv)


def kernel(x, ffn1_norm, ffn1_w_in, ffn1_w_out, mix_norm, ffn2_norm, ffn2_w_in, ffn2_w_out,
           a_w_in, a_b_gate, a_head_norm, a_w_out, kv_norm, kv_w, b_w_q, b_w_out, final_norm):
    B, S, D = x.shape
    depth = ffn1_norm.shape[0]
    n_a = a_w_in.shape[0]
    n_qk = A_HEADS * (D // A_HEADS // 2)
    n_qkv = 2 * n_qk + D
    xt = x.reshape(B * S, D)
    a_w_in_t = jnp.swapaxes(a_w_in, 1, 2)
    kv = None
    for l in range(depth):
        xt = _ffn(xt, ffn1_norm[l], ffn1_w_in, ffn1_w_out, l)
        if l < n_a:
            w_gate = jnp.pad(a_w_in_t[l, n_qkv + D:], ((0, LANES - 2 * A_HEADS), (0, 0)))
            qkv, og, gates = _mlstm_proj(xt, mix_norm[l], a_w_in_t, l, w_gate, n_qk, n_qkv, D,
                                         scale=(n_qk // A_HEADS) ** -0.5)
            xt = _mlstm_mix(qkv, og, gates, a_b_gate[l], a_head_norm[l], xt,
                            a_w_out[l].astype(BF16), B, S)
        else:
            j = l - n_a
            q = _norm_matmul(xt, mix_norm[l], b_w_q, j, 0, D, BF16,
                             scale=(D // B_HEADS) ** -0.5 * LOG2E)
            o = _stick_breaking_core(q, kv, B, S)
            xt = _matmul_residual(o, b_w_out, j, xt)
        xt = _ffn(xt, ffn2_norm[l], ffn2_w_in, ffn2_w_out, l,
                  out_norm=final_norm if l == depth - 1 else None)
        if l == n_a - 1:
            kv = _norm_matmul(xt, kv_norm, kv_w[None], 0, 0, 2 * D, BF16)
    return xt.reshape(B, S, D)
```

```python
import functools

import jax
import jax.numpy as jnp
from jax import lax
from jax.experimental import pallas as pl
from jax.experimental.pallas import tpu as pltpu

EPS = 1e-6
GATE_SOFTCAP = 15.0
A_HEADS = 8
B_HEADS = 16
MLSTM_CHUNK = 256
SB_BLOCK = 256
SB_HEADS_PER_STEP = 4
LANES = 128
LOG2E = 1.4426950408889634
VMEM_LIMIT_BYTES = 56 * 1024 * 1024

F32 = jnp.float32
BF16 = jnp.bfloat16


def _params(*sem):
    return pltpu.CompilerParams(dimension_semantics=sem, vmem_limit_bytes=VMEM_LIMIT_BYTES)


def _tile(n, want):
    t = min(n, want)
    assert n % t == 0, (n, want)
    return t


def _rms_scale(x, g):
    ms = jnp.mean(x * x, axis=-1, keepdims=True)
    return x * lax.rsqrt(ms + EPS) * g


def _log_sigmoid(x):
    return jnp.minimum(x, 0.0) - jnp.log1p(jnp.exp(-jnp.abs(x)))


def _ffn_body(x_ref, g_ref, wg_ref, wu_ref, wo_ref, *rest, has_out_norm):
    if has_out_norm:
        go_ref, o_ref, xn_ref = rest
    else:
        o_ref, xn_ref = rest

    def half_swiglu(xn):
        gate = jnp.dot(xn, wg_ref[...].astype(BF16), preferred_element_type=F32)
        up = jnp.dot(xn, wu_ref[...].astype(BF16), preferred_element_type=F32)
        act = (0.5 * gate * jax.nn.sigmoid(gate) * up).astype(BF16)
        return jnp.dot(act, wo_ref[...].astype(BF16), preferred_element_type=F32)

    @pl.when(pl.program_id(1) == 0)
    def _():
        x = x_ref[...]
        xn = _rms_scale(x, g_ref[...]).astype(BF16)
        xn_ref[...] = xn
        o_ref[...] = x + half_swiglu(xn)

    @pl.when(pl.program_id(1) != 0)
    def _():
        o_ref[...] += half_swiglu(xn_ref[...])

    if has_out_norm:
        @pl.when(pl.program_id(1) == pl.num_programs(1) - 1)
        def _():
            o_ref[...] = _rms_scale(o_ref[...], go_ref[...])


def _ffn(x, g, w_in, w_out, layer, out_norm=None, *, tm=1024, tf=256):
    T, D = x.shape
    F = w_out.shape[1]
    tm, tf = _tile(T, tm), _tile(F, tf)
    nf = F // tf
    extra_specs, extra_args = [], []
    if out_norm is not None:
        extra_specs = [pl.BlockSpec((1, D), lambda i, j: (0, 0))]
        extra_args = [out_norm.reshape(1, D)]
    return pl.pallas_call(
        functools.partial(_ffn_body, has_out_norm=out_norm is not None),
        grid=(T // tm, nf),
        in_specs=[
            pl.BlockSpec((tm, D), lambda i, j: (i, 0)),
            pl.BlockSpec((1, D), lambda i, j: (0, 0)),
            pl.BlockSpec((None, D, tf), lambda i, j: (layer, 0, j)),
            pl.BlockSpec((None, D, tf), lambda i, j: (layer, 0, j + nf)),
            pl.BlockSpec((None, tf, D), lambda i, j: (layer, j, 0)),
        ] + extra_specs,
        out_specs=pl.BlockSpec((tm, D), lambda i, j: (i, 0)),
        out_shape=jax.ShapeDtypeStruct((T, D), F32),
        scratch_shapes=[pltpu.VMEM((tm, D), BF16)],
        compiler_params=_params("parallel", "arbitrary"),
        name="ffn",
    )(x, g.reshape(1, D), w_in, w_in, w_out, *extra_args)


def _norm_matmul_body(x_ref, g_ref, w_ref, o_ref, xn_ref, *, scale, scale_blocks, w_transposed):
    def project(xn):
        contract_w = 1 if w_transposed else 0
        y = lax.dot_general(xn, w_ref[...].astype(BF16), (((1,), (contract_w,)), ((), ())),
                            preferred_element_type=F32)
        if scale != 1.0:
            y = y * jnp.where(pl.program_id(1) < scale_blocks, scale, 1.0)
        o_ref[...] = y.astype(o_ref.dtype)

    @pl.when(pl.program_id(1) == 0)
    def _():
        xn = _rms_scale(x_ref[...], g_ref[...]).astype(BF16)
        xn_ref[...] = xn
        project(xn)

    @pl.when(pl.program_id(1) != 0)
    def _():
        project(xn_ref[...])


def _norm_matmul(x, g, w, layer, col0, n, out_dtype, *, scale=1.0, scale_cols=None,
                 w_transposed=False, tm=1024, tn=1024):
    T, D = x.shape
    tm, tn = _tile(T, tm), _tile(n, tn)
    scale_cols = n if scale_cols is None else scale_cols
    assert col0 % tn == 0 and scale_cols % tn == 0
    cb = col0 // tn
    return pl.pallas_call(
        functools.partial(_norm_matmul_body, scale=float(scale), scale_blocks=scale_cols // tn,
                          w_transposed=w_transposed),
        grid=(T // tm, n // tn),
        in_specs=[
            pl.BlockSpec((tm, D), lambda i, j: (i, 0)),
            pl.BlockSpec((1, D), lambda i, j: (0, 0)),
            (pl.BlockSpec((None, tn, D), lambda i, j: (layer, j + cb, 0)) if w_transposed else
             pl.BlockSpec((None, D, tn), lambda i, j: (layer, 0, j + cb))),
        ],
        out_specs=pl.BlockSpec((tm, tn), lambda i, j: (i, j)),
        out_shape=jax.ShapeDtypeStruct((T, n), out_dtype),
        scratch_shapes=[pltpu.VMEM((tm, D), BF16)],
        compiler_params=_params("parallel", "arbitrary"),
        name="norm_matmul",
    )(x, g.reshape(1, D), w)


def _mlstm_proj_body(x_ref, g_ref, w_ref, wgate_ref, qkv_ref, og_ref, gates_ref, xn_ref, *,
                     scale, scale_blocks, qkv_blocks):
    nt = (((1,), (1,)), ((), ()))
    j = pl.program_id(1)

    def project(xn):
        return lax.dot_general(xn, w_ref[...].astype(BF16), nt, preferred_element_type=F32)

    @pl.when(j == 0)
    def _():
        xn = _rms_scale(x_ref[...], g_ref[...]).astype(BF16)
        xn_ref[...] = xn
        gates_ref[...] = lax.dot_general(xn, wgate_ref[...].astype(BF16), nt,
                                         preferred_element_type=F32)
        qkv_ref[...] = (project(xn) * scale).astype(qkv_ref.dtype)

    @pl.when((j != 0) & (j < qkv_blocks))
    def _():
        y = project(xn_ref[...]) * jnp.where(j < scale_blocks, scale, 1.0)
        qkv_ref[...] = y.astype(qkv_ref.dtype)

    @pl.when(j >= qkv_blocks)
    def _():
        og_ref[...] = project(xn_ref[...])


def _mlstm_proj(x, g, w_t, layer, w_gate, n_q, n_qkv, n_og, *, scale, tm=1024, tn=512):
    T, D = x.shape
    tm = _tile(T, tm)
    assert n_q % tn == 0 and n_qkv % tn == 0 and n_og % tn == 0 and n_q >= tn
    qb, ob = n_qkv // tn, n_og // tn
    return pl.pallas_call(
        functools.partial(_mlstm_proj_body, scale=float(scale), scale_blocks=n_q // tn,
                          qkv_blocks=qb),
        grid=(T // tm, qb + ob),
        in_specs=[
            pl.BlockSpec((tm, D), lambda i, j: (i, 0)),
            pl.BlockSpec((1, D), lambda i, j: (0, 0)),
            pl.BlockSpec((None, tn, D), lambda i, j: (layer, j, 0)),
            pl.BlockSpec((LANES, D), lambda i, j: (0, 0)),
        ],
        out_specs=[
            pl.BlockSpec((tm, tn), lambda i, j: (i, jnp.minimum(j, qb - 1))),
            pl.BlockSpec((tm, tn), lambda i, j: (i, jnp.maximum(j - qb, 0))),
            pl.BlockSpec((tm, LANES), lambda i, j: (i, 0)),
        ],
        out_shape=[jax.ShapeDtypeStruct((T, n_qkv), BF16),
                   jax.ShapeDtypeStruct((T, n_og), F32),
                   jax.ShapeDtypeStruct((T, LANES), F32)],
        scratch_shapes=[pltpu.VMEM((tm, D), BF16)],
        compiler_params=_params("parallel", "arbitrary"),
        name="mlstm_proj",
    )(x, g.reshape(1, D), w_t, w_gate)


def _matmul_residual_body(h_ref, w_ref, x_ref, o_ref):
    o_ref[...] = x_ref[...] + jnp.dot(h_ref[...], w_ref[...].astype(BF16),
                                      preferred_element_type=F32)


def _matmul_residual(h, w, layer, x, *, tm=2048, tn=512):
    T, K = h.shape
    D = x.shape[1]
    tm, tn = _tile(T, tm), _tile(D, tn)
    return pl.pallas_call(
        _matmul_residual_body,
        grid=(T // tm, D // tn),
        in_specs=[
            pl.BlockSpec((tm, K), lambda i, j: (i, 0)),
            pl.BlockSpec((None, K, tn), lambda i, j: (layer, 0, j)),
            pl.BlockSpec((tm, tn), lambda i, j: (i, j)),
        ],
        out_specs=pl.BlockSpec((tm, tn), lambda i, j: (i, j)),
        out_shape=jax.ShapeDtypeStruct((T, D), F32),
        compiler_params=_params("parallel", "arbitrary"),
        name="matmul_residual",
    )(h, w, x)


def _split3(x):
    h1 = x.astype(BF16)
    r1 = x - h1.astype(F32)
    h2 = r1.astype(BF16)
    h3 = (r1 - h2.astype(F32)).astype(BF16)
    return h1, h2, h3


def _mlstm_body(q_ref, k_ref, v_ref, og_ref, gc_ref, gr_ref, bc_ref, br_ref, hn_ref, x_ref, wb_ref,
                o_ref, c_ref, n_ref, m_ref, hs_ref, *, L, H, dqk, dv, nc, out_chunks):
    t = pl.program_id(0)

    @pl.when(t == 0)
    def _():
        hs_ref[...] = jnp.zeros(hs_ref.shape, hs_ref.dtype)

    @pl.when(lax.rem(t, nc) == 0)
    def _():
        c_ref[...] = jnp.zeros(c_ref.shape, F32)
        n_ref[...] = jnp.zeros(n_ref.shape, F32)
        m_ref[...] = jnp.zeros(m_ref.shape, F32)

    hs_prev = hs_ref[...]
    dn = wb_ref.shape[1] // out_chunks

    def out_proj(i):
        cols = slice(i * dn, (i + 1) * dn)
        o_ref[:, cols] = x_ref[:, cols] + jnp.dot(hs_prev, wb_ref[:, cols],
                                                  preferred_element_type=F32)

    row = lax.broadcasted_iota(jnp.int32, (L, L), 0)
    col = lax.broadcasted_iota(jnp.int32, (L, L), 1)
    causal = col <= row
    tri = causal.astype(BF16)

    gcol = gc_ref[...] + bc_ref[...]
    ig_c = GATE_SOFTCAP * jnp.tanh(gcol * (1.0 / GATE_SOFTCAP))
    lf_c = _log_sigmoid(gcol)
    b_c = sum(jnp.dot(tri, p, preferred_element_type=F32) for p in _split3(lf_c))
    grow = gr_ref[...] + br_ref[...]
    ig_r = GATE_SOFTCAP * jnp.tanh(grow * (1.0 / GATE_SOFTCAP))
    lf_r = _log_sigmoid(grow)
    b_r = sum(lax.dot_general(p, tri, (((1,), (1,)), ((), ())), preferred_element_type=F32)
              for p in _split3(lf_r))

    qs = [q_ref[:, h * dqk:(h + 1) * dqk] for h in range(H)]
    ks = [k_ref[:, h * dqk:(h + 1) * dqk] for h in range(H)]
    vs = [v_ref[:, h * dv:(h + 1) * dv] for h in range(H)]
    cts = [c_ref[h] for h in range(H)]
    ss = [lax.dot_general(qs[h], ks[h], (((1,), (1,)), ((), ())), preferred_element_type=F32)
          for h in range(H)]
    qcs = [jnp.dot(qs[h], cts[h].astype(BF16), preferred_element_type=F32) for h in range(H)]

    sws, decs, invs, wvs, cdecs, n_news, m_news = [], [], [], [], [], [], []
    for h in range(H):
        bcol = b_c[:, H + h:H + h + 1]
        icol = ig_c[:, h:h + 1]
        brow = b_r[H + h:H + h + 1, :]
        irow = ig_r[h:h + 1, :]
        m_prev = m_ref[h]
        n_prev = n_ref[h]
        dm = jnp.where(causal, bcol - brow + irow, -jnp.inf)
        m_inter = bcol + m_prev
        m_t = jnp.maximum(jnp.max(dm, axis=1, keepdims=True), m_inter)
        sw = ss[h] * jnp.exp(dm - m_t)
        dec = jnp.exp(m_inter - m_t)
        nq = jnp.sum(qs[h].astype(F32) * n_prev, axis=1, keepdims=True)
        den = jnp.sum(sw, axis=1, keepdims=True) + dec * nq
        invs.append(1.0 / jnp.maximum(jnp.abs(den), jnp.exp(-m_t)))
        sws.append(sw.astype(BF16))
        decs.append(dec)
        b_last = bcol[L - 1:L, :]
        gdec = b_last - bcol + icol
        m_new = jnp.maximum(b_last + m_prev, jnp.max(gdec, axis=0, keepdims=True))
        w = jnp.exp(gdec - m_new)
        cdec = jnp.exp(b_last + m_prev - m_new)
        wvs.append((w * vs[h].astype(F32)).astype(BF16))
        n_news.append(cdec * n_prev + jnp.sum(w * ks[h].astype(F32), axis=0, keepdims=True))
        cdecs.append(cdec)
        m_news.append(m_new)
        for i in range(h * out_chunks // H, (h + 1) * out_chunks // H):
            out_proj(i)

    nums = [jnp.dot(sws[h], vs[h], preferred_element_type=F32) for h in range(H)]
    kvs = [lax.dot_general(ks[h], wvs[h], (((0,), (0,)), ((), ())), preferred_element_type=F32)
           for h in range(H)]

    for h in range(H):
        hd = slice(h * dv, (h + 1) * dv)
        out = (nums[h] + decs[h] * qcs[h]) * invs[h]
        hs = out * lax.rsqrt(jnp.mean(out * out, axis=1, keepdims=True) + EPS)
        hs = hs * hn_ref[:, hd] * jax.nn.sigmoid(og_ref[:, hd])
        hs_ref[:, hd] = hs.astype(hs_ref.dtype)
        c_ref[h] = cdecs[h] * cts[h] + kvs[h]
        n_ref[h] = n_news[h]
        m_ref[h] = m_news[h]


def _mlstm_mix(qkv, og, gates, b_gate, head_norm, x, w_out, B, S):
    H = A_HEADS
    T, D = x.shape
    dv = og.shape[1] // H
    dqk = dv // 2
    L = _tile(S, MLSTM_CHUNK)
    nc = S // L
    n = B * nc
    gw = gates.shape[1]
    gates_t = gates[:, :2 * H].T
    bias_c = jnp.pad(b_gate, (0, gw - 2 * H)).reshape(1, gw)
    kern = functools.partial(_mlstm_body, L=L, H=H, dqk=dqk, dv=dv, nc=nc, out_chunks=8)
    cur = lambda t: jnp.minimum(t, n - 1)
    prev = lambda t: jnp.maximum(t - 1, 0)
    return pl.pallas_call(
        kern,
        grid=(n + 1,),
        in_specs=[
            pl.BlockSpec((L, H * dqk), lambda t: (cur(t), 0)),
            pl.BlockSpec((L, H * dqk), lambda t: (cur(t), 1)),
            pl.BlockSpec((L, H * dv), lambda t: (cur(t), 1)),
            pl.BlockSpec((L, H * dv), lambda t: (cur(t), 0)),
            pl.BlockSpec((L, gw), lambda t: (cur(t), 0)),
            pl.BlockSpec((2 * H, L), lambda t: (0, cur(t))),
            pl.BlockSpec((1, gw), lambda t: (0, 0)),
            pl.BlockSpec((2 * H, 1), lambda t: (0, 0)),
            pl.BlockSpec((1, H * dv), lambda t: (0, 0)),
            pl.BlockSpec((L, D), lambda t: (prev(t), 0)),
            pl.BlockSpec((H * dv, D), lambda t: (0, 0), pipeline_mode=pl.Buffered(1)),
        ],
        out_specs=pl.BlockSpec((L, D), lambda t: (prev(t), 0)),
        out_shape=jax.ShapeDtypeStruct((T, D), F32),
        scratch_shapes=[
            pltpu.VMEM((H, dqk, dv), F32),
            pltpu.VMEM((H, 1, dqk), F32),
            pltpu.VMEM((H, 1, 1), F32),
            pltpu.VMEM((L, H * dv), BF16),
        ],
        compiler_params=_params("arbitrary"),
        name="mlstm_mix",
    )(qkv, qkv, qkv, og, gates, gates_t, bias_c, b_gate.reshape(2 * H, 1),
      head_norm.reshape(1, H * dv), x, w_out)


def _sb_body(q_ref, k_ref, v_ref, o_ref, *, QB, G, dh):
    qi = pl.program_id(2)
    row = lax.broadcasted_iota(jnp.int32, (QB, QB), 0)
    col = lax.broadcasted_iota(jnp.int32, (QB, QB), 1)
    strict = col < row
    later = (row > col).astype(BF16)
    later2 = jnp.concatenate([later, later], axis=0)

    def step(j, state, diag):
        start = pl.multiple_of(j * QB, QB)
        heads = [slice(g * dh, (g + 1) * dh) for g in range(G)]
        zs = [lax.dot_general(q_ref[:, hs], k_ref[pl.ds(start, QB), hs],
                              (((1,), (1,)), ((), ())), preferred_element_type=F32)
              for hs in heads]
        log_sigs, carries, pieces = [], [], []
        for g, z in enumerate(zs):
            neg = jnp.minimum(z, 0.0)
            pos = jnp.maximum(z, 0.0)
            lg = jnp.log(1.0 + jnp.exp2(neg - pos)) * LOG2E
            sp = pos + lg
            if diag:
                sp = jnp.where(strict, sp, 0.0)
            hi = sp.astype(BF16)
            lo = (sp - hi.astype(F32)).astype(BF16)
            log_sigs.append((neg - lg) - state[g][0])
            carries.append(state[g][0] + jnp.sum(sp, axis=1, keepdims=True))
            pieces.append(jnp.concatenate([hi, lo], axis=1))
        s_afters = [jnp.dot(p, later2, preferred_element_type=F32) for p in pieces]
        atts = []
        for g in range(G):
            att = jnp.exp2(log_sigs[g] - s_afters[g])
            if diag:
                att = jnp.where(strict, att, 0.0)
            atts.append(att.astype(BF16))
        new_state = []
        for g in range(G):
            acc = state[g][1] + jnp.dot(atts[g], v_ref[pl.ds(start, QB), heads[g]],
                                        preferred_element_type=F32)
            new_state.append((carries[g], acc))
        return tuple(new_state)

    state = tuple((jnp.zeros((QB, 1), F32), jnp.zeros((QB, dh), F32)) for _ in range(G))
    state = step(qi, state, True)
    state = lax.fori_loop(0, qi, lambda it, st: step(qi - 1 - it, st, False), state)
    o_ref[...] = jnp.concatenate([st[1] for st in state], axis=1).astype(o_ref.dtype)


def _stick_breaking_core(q, kv, B, S):
    H = B_HEADS
    G = SB_HEADS_PER_STEP
    T = B * S
    dh = q.shape[1] // H
    QB = _tile(S, SB_BLOCK)
    nq = S // QB
    kern = functools.partial(_sb_body, QB=QB, G=G, dh=dh)
    return pl.pallas_call(
        kern,
        grid=(B, H // G, nq),
        in_specs=[
            pl.BlockSpec((QB, G * dh), lambda b, h, i: (b * nq + i, h)),
            pl.BlockSpec((S, G * dh), lambda b, h, i: (b, h)),
            pl.BlockSpec((S, G * dh), lambda b, h, i: (b, H // G + h)),
        ],
        out_specs=pl.BlockSpec((QB, G * dh), lambda b, h, i: (b * nq + i, h)),
        out_shape=jax.ShapeDtypeStruct((T, H * dh), BF16),
        compiler_params=_params("parallel", "parallel", "arbitrary"),
        name="stick_breaking_core",
    )(q, kv, kv)


def kernel(x, ffn1_norm, ffn1_w_in, ffn1_w_out, mix_norm, ffn2_norm, ffn2_w_in, ffn2_w_out,
           a_w_in, a_b_gate, a_head_norm, a_w_out, kv_norm, kv_w, b_w_q, b_w_out, final_norm):
    B, S, D = x.shape
    depth = ffn1_norm.shape[0]
    n_a = a_w_in.shape[0]
    n_qk = A_HEADS * (D // A_HEADS // 2)
    n_qkv = 2 * n_qk + D
    xt = x.reshape(B * S, D)
    a_w_in_t = jnp.swapaxes(a_w_in, 1, 2)
    kv = None
    for l in range(depth):
        xt = _ffn(xt, ffn1_norm[l], ffn1_w_in, ffn1_w_out, l)
        if l < n_a:
            w_gate = jnp.pad(a_w_in_t[l, n_qkv + D:], ((0, LANES - 2 * A_HEADS), (0, 0)))
            qkv, og, gates = _mlstm_proj(xt, mix_norm[l], a_w_in_t, l, w_gate, n_qk, n_qkv, D,
                                         scale=(n_qk // A_HEADS) ** -0.5)
            xt = _mlstm_mix(qkv, og, gates, a_b_gate[l], a_head_norm[l], xt,
                            a_w_out[l].astype(BF16), B, S)
        else:
            j = l - n_a
            q = _norm_matmul(xt, mix_norm[l], b_w_q, j, 0, D, BF16,
                             scale=(D // B_HEADS) ** -0.5 * LOG2E)
            o = _stick_breaking_core(q, kv, B, S)
            xt = _matmul_residual(o, b_w_out, j, xt)
        xt = _ffn(xt, ffn2_norm[l], ffn2_w_in, ffn2_w_out, l,
                  out_norm=final_norm if l == depth - 1 else None)
        if l == n_a - 1:
            kv = _norm_matmul(xt, kv_norm, kv_w[None], 0, 0, 2 * D, BF16)
    return xt.reshape(B, S, D)
```

```python
import functools

import jax
import jax.numpy as jnp
from jax import lax
from jax.experimental import pallas as pl
from jax.experimental.pallas import tpu as pltpu

EPS = 1e-6
GATE_SOFTCAP = 15.0
A_HEADS = 8
B_HEADS = 16
MLSTM_CHUNK = 256
SB_BLOCK = 256
SB_HEADS_PER_STEP = 4
LANES = 128
LOG2E = 1.4426950408889634
VMEM_LIMIT_BYTES = 56 * 1024 * 1024

F32 = jnp.float32
BF16 = jnp.bfloat16


def _params(*sem):
    return pltpu.CompilerParams(dimension_semantics=sem, vmem_limit_bytes=VMEM_LIMIT_BYTES)


def _tile(n, want):
    t = min(n, want)
    assert n % t == 0, (n, want)
    return t


def _rms_scale(x, g):
    ms = jnp.mean(x * x, axis=-1, keepdims=True)
    return x * lax.rsqrt(ms + EPS) * g


def _log_sigmoid(x):
    return jnp.minimum(x, 0.0) - jnp.log1p(jnp.exp(-jnp.abs(x)))


def _ffn_body(x_ref, g_ref, wg_ref, wu_ref, wo_ref, *rest, has_out_norm):
    if has_out_norm:
        go_ref, o_ref, xn_ref = rest
    else:
        o_ref, xn_ref = rest

    def half_swiglu(xn):
        gate = jnp.dot(xn, wg_ref[...].astype(BF16), preferred_element_type=F32)
        up = jnp.dot(xn, wu_ref[...].astype(BF16), preferred_element_type=F32)
        act = (0.5 * gate * jax.nn.sigmoid(gate) * up).astype(BF16)
        return jnp.dot(act, wo_ref[...].astype(BF16), preferred_element_type=F32)

    @pl.when(pl.program_id(1) == 0)
    def _():
        x = x_ref[...]
        xn = _rms_scale(x, g_ref[...]).astype(BF16)
        xn_ref[...] = xn
        o_ref[...] = x + half_swiglu(xn)

    @pl.when(pl.program_id(1) != 0)
    def _():
        o_ref[...] += half_swiglu(xn_ref[...])

    if has_out_norm:
        @pl.when(pl.program_id(1) == pl.num_programs(1) - 1)
        def _():
            o_ref[...] = _rms_scale(o_ref[...], go_ref[...])


def _ffn(x, g, w_in, w_out, layer, out_norm=None, *, tm=1024, tf=256):
    T, D = x.shape
    F = w_out.shape[1]
    tm, tf = _tile(T, tm), _tile(F, tf)
    nf = F // tf
    extra_specs, extra_args = [], []
    if out_norm is not None:
        extra_specs = [pl.BlockSpec((1, D), lambda i, j: (0, 0))]
        extra_args = [out_norm.reshape(1, D)]
    return pl.pallas_call(
        functools.partial(_ffn_body, has_out_norm=out_norm is not None),
        grid=(T // tm, nf),
        in_specs=[
            pl.BlockSpec((tm, D), lambda i, j: (i, 0)),
            pl.BlockSpec((1, D), lambda i, j: (0, 0)),
            pl.BlockSpec((None, D, tf), lambda i, j: (layer, 0, j)),
            pl.BlockSpec((None, D, tf), lambda i, j: (layer, 0, j + nf)),
            pl.BlockSpec((None, tf, D), lambda i, j: (layer, j, 0)),
        ] + extra_specs,
        out_specs=pl.BlockSpec((tm, D), lambda i, j: (i, 0)),
        out_shape=jax.ShapeDtypeStruct((T, D), F32),
        scratch_shapes=[pltpu.VMEM((tm, D), BF16)],
        compiler_params=_params("parallel", "arbitrary"),
        name="ffn",
    )(x, g.reshape(1, D), w_in, w_in, w_out, *extra_args)


def _norm_matmul_body(x_ref, g_ref, w_ref, o_ref, xn_ref, *, scale, scale_blocks, w_transposed):
    def project(xn):
        contract_w = 1 if w_transposed else 0
        y = lax.dot_general(xn, w_ref[...].astype(BF16), (((1,), (contract_w,)), ((), ())),
                            preferred_element_type=F32)
        if scale != 1.0:
            y = y * jnp.where(pl.program_id(1) < scale_blocks, scale, 1.0)
        o_ref[...] = y.astype(o_ref.dtype)

    @pl.when(pl.program_id(1) == 0)
    def _():
        xn = _rms_scale(x_ref[...], g_ref[...]).astype(BF16)
        xn_ref[...] = xn
        project(xn)

    @pl.when(pl.program_id(1) != 0)
    def _():
        project(xn_ref[...])


def _norm_matmul(x, g, w, layer, col0, n, out_dtype, *, scale=1.0, scale_cols=None,
                 w_transposed=False, tm=1024, tn=1024):
    T, D = x.shape
    tm, tn = _tile(T, tm), _tile(n, tn)
    scale_cols = n if scale_cols is None else scale_cols
    assert col0 % tn == 0 and scale_cols % tn == 0
    cb = col0 // tn
    return pl.pallas_call(
        functools.partial(_norm_matmul_body, scale=float(scale), scale_blocks=scale_cols // tn,
                          w_transposed=w_transposed),
        grid=(T // tm, n // tn),
        in_specs=[
            pl.BlockSpec((tm, D), lambda i, j: (i, 0)),
            pl.BlockSpec((1, D), lambda i, j: (0, 0)),
            (pl.BlockSpec((None, tn, D), lambda i, j: (layer, j + cb, 0)) if w_transposed else
             pl.BlockSpec((None, D, tn), lambda i, j: (layer, 0, j + cb))),
        ],
        out_specs=pl.BlockSpec((tm, tn), lambda i, j: (i, j)),
        out_shape=jax.ShapeDtypeStruct((T, n), out_dtype),
        scratch_shapes=[pltpu.VMEM((tm, D), BF16)],
        compiler_params=_params("parallel", "arbitrary"),
        name="norm_matmul",
    )(x, g.reshape(1, D), w)


def _mlstm_proj_body(x_ref, g_ref, w_ref, wgate_ref, qkv_ref, og_ref, gates_ref, xn_ref, *,
                     scale, scale_blocks, qkv_blocks):
    nt = (((1,), (1,)), ((), ()))
    j = pl.program_id(1)

    def project(xn):
        return lax.dot_general(xn, w_ref[...].astype(BF16), nt, preferred_element_type=F32)

    @pl.when(j == 0)
    def _():
        xn = _rms_scale(x_ref[...], g_ref[...]).astype(BF16)
        xn_ref[...] = xn
        gates_ref[...] = lax.dot_general(xn, wgate_ref[...].astype(BF16), nt,
                                         preferred_element_type=F32)
        qkv_ref[...] = (project(xn) * scale).astype(qkv_ref.dtype)

    @pl.when((j != 0) & (j < qkv_blocks))
    def _():
        y = project(xn_ref[...]) * jnp.where(j < scale_blocks, scale, 1.0)
        qkv_ref[...] = y.astype(qkv_ref.dtype)

    @pl.when(j >= qkv_blocks)
    def _():
        og_ref[...] = project(xn_ref[...])


def _mlstm_proj(x, g, w_t, layer, w_gate, n_q, n_qkv, n_og, *, scale, tm=1024, tn=512):
    T, D = x.shape
    tm = _tile(T, tm)
    assert n_q % tn == 0 and n_qkv % tn == 0 and n_og % tn == 0 and n_q >= tn
    qb, ob = n_qkv // tn, n_og // tn
    return pl.pallas_call(
        functools.partial(_mlstm_proj_body, scale=float(scale), scale_blocks=n_q // tn,
                          qkv_blocks=qb),
        grid=(T // tm, qb + ob),
        in_specs=[
            pl.BlockSpec((tm, D), lambda i, j: (i, 0)),
            pl.BlockSpec((1, D), lambda i, j: (0, 0)),
            pl.BlockSpec((None, tn, D), lambda i, j: (layer, j, 0)),
            pl.BlockSpec((LANES, D), lambda i, j: (0, 0)),
        ],
        out_specs=[
            pl.BlockSpec((tm, tn), lambda i, j: (i, jnp.minimum(j, qb - 1))),
            pl.BlockSpec((tm, tn), lambda i, j: (i, jnp.maximum(j - qb, 0))),
            pl.BlockSpec((tm, LANES), lambda i, j: (i, 0)),
        ],
        out_shape=[jax.ShapeDtypeStruct((T, n_qkv), BF16),
                   jax.ShapeDtypeStruct((T, n_og), F32),
                   jax.ShapeDtypeStruct((T, LANES), F32)],
        scratch_shapes=[pltpu.VMEM((tm, D), BF16)],
        compiler_params=_params("parallel", "arbitrary"),
        name="mlstm_proj",
    )(x, g.reshape(1, D), w_t, w_gate)


def _matmul_residual_body(h_ref, w_ref, x_ref, o_ref):
    o_ref[...] = x_ref[...] + jnp.dot(h_ref[...], w_ref[...].astype(BF16),
                                      preferred_element_type=F32)


def _matmul_residual(h, w, layer, x, *, tm=2048, tn=512):
    T, K = h.shape
    D = x.shape[1]
    tm, tn = _tile(T, tm), _tile(D, tn)
    return pl.pallas_call(
        _matmul_residual_body,
        grid=(T // tm, D // tn),
        in_specs=[
            pl.BlockSpec((tm, K), lambda i, j: (i, 0)),
            pl.BlockSpec((None, K, tn), lambda i, j: (layer, 0, j)),
            pl.BlockSpec((tm, tn), lambda i, j: (i, j)),
        ],
        out_specs=pl.BlockSpec((tm, tn), lambda i, j: (i, j)),
        out_shape=jax.ShapeDtypeStruct((T, D), F32),
        compiler_params=_params("parallel", "arbitrary"),
        name="matmul_residual",
    )(h, w, x)


def _split3(x):
    h1 = x.astype(BF16)
    r1 = x - h1.astype(F32)
    h2 = r1.astype(BF16)
    h3 = (r1 - h2.astype(F32)).astype(BF16)
    return h1, h2, h3


def _mlstm_body(q_ref, k_ref, v_ref, og_ref, gc_ref, gr_ref, bc_ref, br_ref, hn_ref, x_ref, wb_ref,
                o_ref, c_ref, n_ref, m_ref, hs_ref, *, L, H, dqk, dv, nc, out_chunks):
    t = pl.program_id(0)

    @pl.when(t == 0)
    def _():
        hs_ref[...] = jnp.zeros(hs_ref.shape, hs_ref.dtype)

    @pl.when(lax.rem(t, nc) == 0)
    def _():
        c_ref[...] = jnp.zeros(c_ref.shape, F32)
        n_ref[...] = jnp.zeros(n_ref.shape, F32)
        m_ref[...] = jnp.zeros(m_ref.shape, F32)

    hs_prev = hs_ref[...]
    dn = wb_ref.shape[1] // out_chunks

    def out_proj(i):
        cols = slice(i * dn, (i + 1) * dn)
        o_ref[:, cols] = x_ref[:, cols] + jnp.dot(hs_prev, wb_ref[:, cols],
                                                  preferred_element_type=F32)

    row = lax.broadcasted_iota(jnp.int32, (L, L), 0)
    col = lax.broadcasted_iota(jnp.int32, (L, L), 1)
    causal = col <= row
    tri = causal.astype(BF16)

    gcol = gc_ref[...] + bc_ref[...]
    ig_c = GATE_SOFTCAP * jnp.tanh(gcol * (1.0 / GATE_SOFTCAP))
    lf_c = _log_sigmoid(gcol)
    b_c = sum(jnp.dot(tri, p, preferred_element_type=F32) for p in _split3(lf_c))
    grow = gr_ref[...] + br_ref[...]
    ig_r = GATE_SOFTCAP * jnp.tanh(grow * (1.0 / GATE_SOFTCAP))
    lf_r = _log_sigmoid(grow)
    b_r = sum(lax.dot_general(p, tri, (((1,), (1,)), ((), ())), preferred_element_type=F32)
              for p in _split3(lf_r))

    qs = [q_ref[:, h * dqk:(h + 1) * dqk] for h in range(H)]
    ks = [k_ref[:, h * dqk:(h + 1) * dqk] for h in range(H)]
    vs = [v_ref[:, h * dv:(h + 1) * dv] for h in range(H)]
    cts = [c_ref[h] for h in range(H)]
    ss = [lax.dot_general(qs[h], ks[h], (((1,), (1,)), ((), ())), preferred_element_type=F32)
          for h in range(H)]
    qcs = [jnp.dot(qs[h], cts[h].astype(BF16), preferred_element_type=F32) for h in range(H)]

    sws, decs, invs, wvs, cdecs, n_news, m_news = [], [], [], [], [], [], []
    for h in range(H):
        bcol = b_c[:, H + h:H + h + 1]
        icol = ig_c[:, h:h + 1]
        brow = b_r[H + h:H + h + 1, :]
        irow = ig_r[h:h + 1, :]
        m_prev = m_ref[h]
        n_prev = n_ref[h]
        dm = jnp.where(causal, bcol - brow + irow, -jnp.inf)
        m_inter = bcol + m_prev
        m_t = jnp.maximum(jnp.max(dm, axis=1, keepdims=True), m_inter)
        sw = ss[h] * jnp.exp(dm - m_t)
        dec = jnp.exp(m_inter - m_t)
        nq = jnp.sum(qs[h].astype(F32) * n_prev, axis=1, keepdims=True)
        den = jnp.sum(sw, axis=1, keepdims=True) + dec * nq
        invs.append(1.0 / jnp.maximum(jnp.abs(den), jnp.exp(-m_t)))
        sws.append(sw.astype(BF16))
        decs.append(dec)
        b_last = bcol[L - 1:L, :]
        gdec = b_last - bcol + icol
        m_new = jnp.maximum(b_last + m_prev, jnp.max(gdec, axis=0, keepdims=True))
        w = jnp.exp(gdec - m_new)
        cdec = jnp.exp(b_last + m_prev - m_new)
        wvs.append((w * vs[h].astype(F32)).astype(BF16))
        n_news.append(cdec * n_prev + jnp.sum(w * ks[h].astype(F32), axis=0, keepdims=True))
        cdecs.append(cdec)
        m_news.append(m_new)
        for i in range(h * out_chunks // H, (h + 1) * out_chunks // H):
            out_proj(i)

    nums = [jnp.dot(sws[h], vs[h], preferred_element_type=F32) for h in range(H)]
    kvs = [lax.dot_general(ks[h], wvs[h], (((0,), (0,)), ((), ())), preferred_element_type=F32)
           for h in range(H)]

    for h in range(H):
        hd = slice(h * dv, (h + 1) * dv)
        out = (nums[h] + decs[h] * qcs[h]) * invs[h]
        hs = out * lax.rsqrt(jnp.mean(out * out, axis=1, keepdims=True) + EPS)
        hs = hs * hn_ref[:, hd] * jax.nn.sigmoid(og_ref[:, hd])
        hs_ref[:, hd] = hs.astype(hs_ref.dtype)
        c_ref[h] = cdecs[h] * cts[h] + kvs[h]
        n_ref[h] = n_news[h]
        m_ref[h] = m_news[h]


def _mlstm_mix(qkv, og, gates, b_gate, head_norm, x, w_out, B, S):
    H = A_HEADS
    T, D = x.shape
    dv = og.shape[1] // H
    dqk = dv // 2
    L = _tile(S, MLSTM_CHUNK)
    nc = S // L
    n = B * nc
    gw = gates.shape[1]
    gates_t = gates[:, :2 * H].T
    bias_c = jnp.pad(b_gate, (0, gw - 2 * H)).reshape(1, gw)
    kern = functools.partial(_mlstm_body, L=L, H=H, dqk=dqk, dv=dv, nc=nc, out_chunks=8)
    cur = lambda t: jnp.minimum(t, n - 1)
    prev = lambda t: jnp.maximum(t - 1, 0)
    return pl.pallas_call(
        kern,
        grid=(n + 1,),
        in_specs=[
            pl.BlockSpec((L, H * dqk), lambda t: (cur(t), 0)),
            pl.BlockSpec((L, H * dqk), lambda t: (cur(t), 1)),
            pl.BlockSpec((L, H * dv), lambda t: (cur(t), 1)),
            pl.BlockSpec((L, H * dv), lambda t: (cur(t), 0)),
            pl.BlockSpec((L, gw), lambda t: (cur(t), 0)),
            pl.BlockSpec((2 * H, L), lambda t: (0, cur(t))),
            pl.BlockSpec((1, gw), lambda t: (0, 0)),
            pl.BlockSpec((2 * H, 1), lambda t: (0, 0)),
            pl.BlockSpec((1, H * dv), lambda t: (0, 0)),
            pl.BlockSpec((L, D), lambda t: (prev(t), 0)),
            pl.BlockSpec((H * dv, D), lambda t: (0, 0), pipeline_mode=pl.Buffered(1)),
        ],
        out_specs=pl.BlockSpec((L, D), lambda t: (prev(t), 0)),
        out_shape=jax.ShapeDtypeStruct((T, D), F32),
        scratch_shapes=[
            pltpu.VMEM((H, dqk, dv), F32),
            pltpu.VMEM((H, 1, dqk), F32),
            pltpu.VMEM((H, 1, 1), F32),
            pltpu.VMEM((L, H * dv), BF16),
        ],
        compiler_params=_params("arbitrary"),
        name="mlstm_mix",
    )(qkv, qkv, qkv, og, gates, gates_t, bias_c, b_gate.reshape(2 * H, 1),
      head_norm.reshape(1, H * dv), x, w_out)


def _sb_body(q_ref, k_ref, v_ref, o_ref, acc_ref, carry_ref, *, QB, G, dh):
    qi = pl.program_id(2)
    row = lax.broadcasted_iota(jnp.int32, (QB, QB), 0)
    col = lax.broadcasted_iota(jnp.int32, (QB, QB), 1)
    strict = col < row
    later = (row > col).astype(BF16)
    later2 = jnp.concatenate([later, later], axis=0)

    def step(j, diag):
        start = pl.multiple_of(j * QB, QB)
        heads = [slice(g * dh, (g + 1) * dh) for g in range(G)]
        zs = [lax.dot_general(q_ref[:, hs], k_ref[pl.ds(start, QB), hs],
                              (((1,), (1,)), ((), ())), preferred_element_type=F32)
              for hs in heads]
        log_sigs, pieces = [], []
        for g, z in enumerate(zs):
            neg = jnp.minimum(z, 0.0)
            pos = jnp.maximum(z, 0.0)
            lg = jnp.log(1.0 + jnp.exp2(neg - pos)) * LOG2E
            sp = pos + lg
            if diag:
                sp = jnp.where(strict, sp, 0.0)
            hi = sp.astype(BF16)
            lo = (sp - hi.astype(F32)).astype(BF16)
            row_sum = jnp.sum(sp, axis=1, keepdims=True)
            if diag:
                log_sigs.append(neg - lg)
                carry_ref[g] = row_sum
            else:
                carry = carry_ref[g]
                log_sigs.append((neg - lg) - carry)
                carry_ref[g] = carry + row_sum
            pieces.append(jnp.concatenate([hi, lo], axis=1))
        s_afters = [jnp.dot(p, later2, preferred_element_type=F32) for p in pieces]
        atts = []
        for g in range(G):
            att = jnp.exp2(log_sigs[g] - s_afters[g])
            if diag:
                att = jnp.where(strict, att, 0.0)
            atts.append(att.astype(BF16))
        for g in range(G):
            part = jnp.dot(atts[g], v_ref[pl.ds(start, QB), heads[g]],
                           preferred_element_type=F32)
            if diag:
                acc_ref[:, heads[g]] = part
            else:
                acc_ref[:, heads[g]] += part

    step(qi, True)

    def body(it, c):
        step(qi - 1 - it, False)
        return c

    lax.fori_loop(0, qi, body, 0)
    o_ref[...] = acc_ref[...].astype(o_ref.dtype)


def _stick_breaking_core(q, kv, B, S):
    H = B_HEADS
    G = SB_HEADS_PER_STEP
    T = B * S
    dh = q.shape[1] // H
    QB = _tile(S, SB_BLOCK)
    nq = S // QB
    kern = functools.partial(_sb_body, QB=QB, G=G, dh=dh)
    return pl.pallas_call(
        kern,
        grid=(B, H // G, nq),
        in_specs=[
            pl.BlockSpec((QB, G * dh), lambda b, h, i: (b * nq + i, h)),
            pl.BlockSpec((S, G * dh), lambda b, h, i: (b, h)),
            pl.BlockSpec((S, G * dh), lambda b, h, i: (b, H // G + h)),
        ],
        out_specs=pl.BlockSpec((QB, G * dh), lambda b, h, i: (b * nq + i, h)),
        out_shape=jax.ShapeDtypeStruct((T, H * dh), BF16),
        scratch_shapes=[pltpu.VMEM((QB, G * dh), F32), pltpu.VMEM((G, QB, 1), F32)],
        compiler_params=_params("parallel", "parallel", "arbitrary"),
        name="stick_breaking_core",
    )(q, kv, kv)


def kernel(x, ffn1_norm, ffn1_w_in, ffn1_w_out, mix_norm, ffn2_norm, ffn2_w_in, ffn2_w_out,
           a_w_in, a_b_gate, a_head_norm, a_w_out, kv_norm, kv_w, b_w_q, b_w_out, final_norm):
    B, S, D = x.shape
    depth = ffn1_norm.shape[0]
    n_a = a_w_in.shape[0]
    n_qk = A_HEADS * (D // A_HEADS // 2)
    n_qkv = 2 * n_qk + D
    xt = x.reshape(B * S, D)
    a_w_in_t = jnp.swapaxes(a_w_in, 1, 2)
    kv = None
    for l in range(depth):
        xt = _ffn(xt, ffn1_norm[l], ffn1_w_in, ffn1_w_out, l)
        if l < n_a:
            w_gate = jnp.pad(a_w_in_t[l, n_qkv + D:], ((0, LANES - 2 * A_HEADS), (0, 0)))
            qkv, og, gates = _mlstm_proj(xt, mix_norm[l], a_w_in_t, l, w_gate, n_qk, n_qkv, D,
                                         scale=(n_qk // A_HEADS) ** -0.5)
            xt = _mlstm_mix(qkv, og, gates, a_b_gate[l], a_head_norm[l], xt,
                            a_w_out[l].astype(BF16), B, S)
        else:
            j = l - n_a
            q = _norm_matmul(xt, mix_norm[l], b_w_q, j, 0, D, BF16,
                             scale=(D // B_HEADS) ** -0.5 * LOG2E)
            o = _stick_breaking_core(q, kv, B, S)
            xt = _matmul_residual(o, b_w_out, j, xt)
        xt = _ffn(xt, ffn2_norm[l], ffn2_w_in, ffn2_w_out, l,
                  out_norm=final_norm if l == depth - 1 else None)
        if l == n_a - 1:
            kv = _norm_matmul(xt, kv_norm, kv_w[None], 0, 0, 2 * D, BF16)
    return xt.reshape(B, S, D)
```
